```python
import math
import jax, jax.numpy as jnp
from jax import lax
import numpy as np

D_MODEL = 1024
BATCH = 2
SEQ = 8192
DEPTH = 1
DEC_BATCH = 4
DEC_SEQ = 4096
PAST_LEN = 128

D_FF = 2816
FOURIER_GROUPS = 4
FOURIER_GROUP_WIDTH = 256
D_FOURIER = FOURIER_GROUPS * FOURIER_GROUP_WIDTH
SSM_HEADS = 32
SSM_HEAD_DIM = 64
D_INNER = SSM_HEADS * SSM_HEAD_DIM
SSM_GROUPS = 8
D_STATE = 128
CONV_WIDTH = 5
CONV_PAD = CONV_WIDTH // 2
CHUNK = 128
CONV_DIM = D_INNER + 2 * SSM_GROUPS * D_STATE
OFF_F = D_FOURIER
OFF_Z = OFF_F + D_INNER
OFF_XBC = OFF_Z + CONV_DIM
OFF_DT = OFF_XBC + 2 * SSM_HEADS
N_IN = OFF_DT + 2 * D_MODEL
EPS = 1e-6

kernel_name = "hybrid_fourier_ssd_macaron_encoder"


def rmsnorm(x, g):
    xf = x.astype(jnp.float32)
    y = xf * lax.rsqrt(jnp.mean(xf * xf, axis=-1, keepdims=True) + EPS) * g.astype(jnp.float32)
    return y.astype(x.dtype)


def swiglu(h, w_in, w_out):
    gu = h @ w_in
    gate, up = jnp.split(gu, 2, axis=-1)
    return (jax.nn.silu(gate) * up) @ w_out


def centred_depthwise_conv(u, w, b):
    c = u.shape[-1]
    out = lax.conv_general_dilated(
        u, w.astype(u.dtype)[:, None, :], window_strides=(1,), padding=[(CONV_PAD, CONV_PAD)],
        dimension_numbers=("NWC", "WIO", "NWC"), feature_group_count=c)
    return out + b.astype(u.dtype)


def ssd_scan(xs, dt, A, Bm, Cm):
    b, l, h, p = xs.shape
    g, n = Bm.shape[2], Bm.shape[3]
    r = h // g
    c = l // CHUNK
    x = (xs.astype(jnp.float32) * dt[..., None]).reshape(b, c, CHUNK, g, r, p)
    a = (dt * A).reshape(b, c, CHUNK, g, r).transpose(0, 3, 4, 1, 2)
    Bc = Bm.astype(jnp.float32).reshape(b, c, CHUNK, g, n)
    Cc = Cm.astype(jnp.float32).reshape(b, c, CHUNK, g, n)
    a_cum = jnp.cumsum(a, axis=-1)
    mask = jnp.tril(jnp.ones((CHUNK, CHUNK), dtype=bool))
    diff = a_cum[..., :, None] - a_cum[..., None, :]
    Lmat = jnp.exp(jnp.where(mask, diff, -jnp.inf))
    cb = jnp.einsum('bclgn,bcsgn->bgcls', Cc, Bc)
    y_diag = jnp.einsum('bgrcls,bcsgrp->bclgrp', cb[:, :, None] * Lmat, x)
    decay_states = jnp.exp(a_cum[..., -1:] - a_cum)
    states = jnp.einsum('bclgn,bgrcl,bclgrp->bcgrpn', Bc, decay_states, x)
    chunk_decay = jnp.exp(a_cum[..., -1])

    def step(hc, inp):
        s, d = inp
        return hc * d[..., None, None] + s, hc

    h0 = jnp.zeros((b, g, r, p, n), jnp.float32)
    _, prev = lax.scan(step, h0, (jnp.moveaxis(states, 1, 0), jnp.moveaxis(chunk_decay, -1, 0)))
    prev = jnp.moveaxis(prev, 0, 1)
    y_off = jnp.einsum('bclgn,bcgrpn,bgrcl->bclgrp', Cc, prev, jnp.exp(a_cum))
    return (y_diag + y_off).reshape(b, l, h, p).astype(xs.dtype)


def token_mixer(h, w_in, conv_w, conv_b, dt_bias, a_log, d_skip, ssm_norm_g,
                w_branch_f, w_branch_m, w_out):
    b, l, _ = h.shape
    proj = h @ w_in
    u_f = proj[..., :OFF_F]
    z = proj[..., OFF_F:OFF_Z]
    xbc = proj[..., OFF_Z:OFF_XBC]
    dt_raw = proj[..., OFF_XBC:OFF_DT]
    gates = proj[..., OFF_DT:]

    uf = u_f.astype(jnp.float32).reshape(b, l, FOURIER_GROUPS, FOURIER_GROUP_WIDTH)
    yf = jnp.fft.fft2(uf, axes=(1, 3), norm="ortho").real.reshape(b, l, D_FOURIER).astype(h.dtype)
    yf = yf @ w_branch_f

    xbc = jax.nn.silu(centred_depthwise_conv(xbc, conv_w, conv_b))
    xs = xbc[..., :D_INNER].reshape(b, l, SSM_HEADS, SSM_HEAD_DIM)
    Bm = xbc[..., D_INNER:D_INNER + SSM_GROUPS * D_STATE].reshape(b, l, SSM_GROUPS, D_STATE)
    Cm = xbc[..., D_INNER + SSM_GROUPS * D_STATE:].reshape(b, l, SSM_GROUPS, D_STATE)
    dt = jax.nn.softplus(dt_raw.astype(jnp.float32).reshape(b, l, 2, SSM_HEADS)
                         + dt_bias.astype(jnp.float32))
    A = -jnp.exp(a_log.astype(jnp.float32))
    y_fwd = ssd_scan(xs, dt[:, :, 0], A[0], Bm, Cm)
    flip = lambda t: jnp.flip(t, axis=1)
    y_bwd = flip(ssd_scan(flip(xs), flip(dt[:, :, 1]), A[1], flip(Bm), flip(Cm)))
    y = (y_fwd + y_bwd + xs * d_skip[:, None]).reshape(b, l, D_INNER)
    y = rmsnorm(y * jax.nn.silu(z), ssm_norm_g)
    ym = y @ w_branch_m

    g = jax.nn.sigmoid(gates)
    merged = g[..., :D_MODEL] * yf + g[..., D_MODEL:] * ym
    return merged @ w_out


def encoder_layer(x, ffn1_pre_g, ffn1_post_g, ffn1_w_in, ffn1_w_out,
                  mix_pre_g, mix_post_g, w_in, conv_w, conv_b, dt_bias, a_log, d_skip,
                  ssm_norm_g, w_branch_f, w_branch_m, w_out,
                  ffn2_pre_g, ffn2_post_g, ffn2_w_in, ffn2_w_out):
    x = x + 0.5 * rmsnorm(swiglu(rmsnorm(x, ffn1_pre_g), ffn1_w_in, ffn1_w_out), ffn1_post_g)
    m = token_mixer(rmsnorm(x, mix_pre_g), w_in, conv_w, conv_b, dt_bias, a_log, d_skip,
                    ssm_norm_g, w_branch_f, w_branch_m, w_out)
    x = x + rmsnorm(m, mix_post_g)
    x = x + 0.5 * rmsnorm(swiglu(rmsnorm(x, ffn2_pre_g), ffn2_w_in, ffn2_w_out), ffn2_post_g)
    return x


def setup_inputs(seed: int = 0) -> dict:
    key = jax.random.key(seed)
    ks = iter(jax.random.split(key, 32))
    f32 = jnp.float32
    nrm = lambda shape, scale: jax.random.normal(next(ks), shape, f32) * scale
    gain = lambda: 1.0 + 0.05 * jax.random.normal(next(ks), (DEPTH, D_MODEL), f32)
    x_prompt = jax.random.normal(next(ks), (BATCH, SEQ, D_MODEL), f32)
    x_sample = jax.random.normal(next(ks), (DEC_BATCH, DEC_SEQ, D_MODEL), f32)
    ffn1_pre_g = gain()
    ffn1_post_g = gain()
    ffn1_w_in = nrm((DEPTH, D_MODEL, 2 * D_FF), D_MODEL ** -0.5)
    ffn1_w_out = nrm((DEPTH, D_FF, D_MODEL), D_FF ** -0.5)
    mix_pre_g = gain()
    mix_post_g = gain()
    w_in = nrm((DEPTH, D_MODEL, N_IN), D_MODEL ** -0.5)
    conv_w = nrm((DEPTH, CONV_WIDTH, CONV_DIM), CONV_WIDTH ** -0.5)
    conv_b = nrm((DEPTH, CONV_DIM), 0.01)
    dt0 = jnp.exp(jax.random.uniform(next(ks), (DEPTH, 2, SSM_HEADS), f32,
                                     math.log(1e-3), math.log(1e-1)))
    dt_bias = dt0 + jnp.log(-jnp.expm1(-dt0))
    a_log = jnp.log(jax.random.uniform(next(ks), (DEPTH, 2, SSM_HEADS), f32, 1.0, 16.0))
    d_skip = 1.0 + 0.1 * jax.random.normal(next(ks), (DEPTH, SSM_HEADS), f32)
    ssm_norm_g = 1.0 + 0.05 * jax.random.normal(next(ks), (DEPTH, D_INNER), f32)
    w_branch_f = nrm((DEPTH, D_FOURIER, D_MODEL), D_FOURIER ** -0.5)
    w_branch_m = nrm((DEPTH, D_INNER, D_MODEL), D_INNER ** -0.5)
    w_out = nrm((DEPTH, D_MODEL, D_MODEL), D_MODEL ** -0.5)
    ffn2_pre_g = gain()
    ffn2_post_g = gain()
    ffn2_w_in = nrm((DEPTH, D_MODEL, 2 * D_FF), D_MODEL ** -0.5)
    ffn2_w_out = nrm((DEPTH, D_FF, D_MODEL), D_FF ** -0.5)
    return {"x_prompt": x_prompt, "x_sample": x_sample,
            "ffn1_pre_g": ffn1_pre_g, "ffn1_post_g": ffn1_post_g,
            "ffn1_w_in": ffn1_w_in, "ffn1_w_out": ffn1_w_out,
            "mix_pre_g": mix_pre_g, "mix_post_g": mix_post_g, "w_in": w_in,
            "conv_w": conv_w, "conv_b": conv_b, "dt_bias": dt_bias, "a_log": a_log,
            "d_skip": d_skip, "ssm_norm_g": ssm_norm_g, "w_branch_f": w_branch_f,
            "w_branch_m": w_branch_m, "w_out": w_out,
            "ffn2_pre_g": ffn2_pre_g, "ffn2_post_g": ffn2_post_g,
            "ffn2_w_in": ffn2_w_in, "ffn2_w_out": ffn2_w_out}


def reference(x_prompt, x_sample, ffn1_pre_g, ffn1_post_g, ffn1_w_in, ffn1_w_out,
              mix_pre_g, mix_post_g, w_in, conv_w, conv_b, dt_bias, a_log, d_skip,
              ssm_norm_g, w_branch_f, w_branch_m, w_out,
              ffn2_pre_g, ffn2_post_g, ffn2_w_in, ffn2_w_out):
    y_prompt = x_prompt
    y_sample = x_sample
    for layer in range(DEPTH):
        params = (ffn1_pre_g[layer], ffn1_post_g[layer], ffn1_w_in[layer], ffn1_w_out[layer],
                  mix_pre_g[layer], mix_post_g[layer], w_in[layer], conv_w[layer], conv_b[layer],
                  dt_bias[layer], a_log[layer], d_skip[layer], ssm_norm_g[layer],
                  w_branch_f[layer], w_branch_m[layer], w_out[layer],
                  ffn2_pre_g[layer], ffn2_post_g[layer], ffn2_w_in[layer], ffn2_w_out[layer])
        y_prompt = encoder_layer(y_prompt, *params)
        y_sample = encoder_layer(y_sample, *params)
    return (y_prompt, y_sample)
```

```python
import functools
import math

import numpy as np
import jax
import jax.numpy as jnp
from jax import lax
from jax.experimental import pallas as pl
from jax.experimental.pallas import tpu as pltpu

F32 = jnp.float32
BF16 = jnp.bfloat16

D_MODEL = 1024
D_FF = 2816
FOURIER_GROUP_WIDTH = 256
D_FOURIER = 1024
SSM_HEADS = 32
SSM_HEAD_DIM = 64
D_INNER = SSM_HEADS * SSM_HEAD_DIM
SSM_GROUPS = 8
D_STATE = 128
CONV_WIDTH = 5
CONV_PAD = CONV_WIDTH // 2
CHUNK = 128
CONV_DIM = D_INNER + 2 * SSM_GROUPS * D_STATE
EPS = 1e-6

LANES = 128
MXU_DIM = 256
BF16_SUBLANES = 16
VMEM_LIMIT = 56 * 1024 * 1024

P_Z = 0
P_XBC = D_INNER
P_GATE = D_INNER + CONV_DIM
P_MAIN = P_GATE + 2 * D_MODEL
DT_PAD = LANES


def _params(sem):
    return pltpu.CompilerParams(dimension_semantics=sem, vmem_limit_bytes=VMEM_LIMIT)


def _rms(x, g):
    return x * lax.rsqrt(jnp.mean(x * x, axis=-1, keepdims=True) + EPS) * g


def _resident(shape):
    nd = len(shape)
    return pl.BlockSpec(shape, lambda *_: (0,) * nd, pipeline_mode=pl.Buffered(1))


def _ffn_body(x_ref, pre_ref, post_ref, win_ref, wout_ref, o_ref, h_ref, *, tf):
    x = x_ref[...]
    h_ref[...] = _rms(x, pre_ref[...]).astype(BF16)
    acc = None
    for c in range(D_FF // tf):
        h = h_ref[...]
        gate = jnp.dot(h, win_ref[:, c * tf:(c + 1) * tf], preferred_element_type=F32)
        up = jnp.dot(h, win_ref[:, D_FF + c * tf:D_FF + (c + 1) * tf], preferred_element_type=F32)
        act = (gate * jax.nn.sigmoid(gate) * up).astype(BF16)
        part = jnp.dot(act, wout_ref[c * tf:(c + 1) * tf, :], preferred_element_type=F32)
        acc = part if acc is None else acc + part
    o_ref[...] = x + 0.5 * _rms(acc, post_ref[...])


def _ffn_block(x, pre_g, post_g, w_in, w_out, *, tm=512, tf=256):
    t = x.shape[0]
    assert t % tm == 0 and D_FF % tf == 0
    return pl.pallas_call(
        functools.partial(_ffn_body, tf=tf),
        grid=(t // tm,),
        in_specs=[pl.BlockSpec((tm, D_MODEL), lambda i: (i, 0)),
                  _resident((1, D_MODEL)), _resident((1, D_MODEL)),
                  _resident((D_MODEL, 2 * D_FF)), _resident((D_FF, D_MODEL))],
        out_specs=pl.BlockSpec((tm, D_MODEL), lambda i: (i, 0)),
        out_shape=jax.ShapeDtypeStruct((t, D_MODEL), F32),
        scratch_shapes=[pltpu.VMEM((tm, D_MODEL), BF16)],
        compiler_params=_params(("parallel",)),
        name="ffn_block",
    )(x, pre_g, post_g, w_in, w_out)


def _norm_matmul_body(x_ref, g_ref, w_ref, o_ref, h_ref):
    @pl.when(pl.program_id(1) == 0)
    def _():
        h_ref[...] = _rms(x_ref[...], g_ref[...]).astype(BF16)

    o_ref[...] = jnp.dot(h_ref[...], w_ref[...], preferred_element_type=F32).astype(o_ref.dtype)


def _norm_matmul(x, g, w, out_dtype, *, tm=1024, tn=1024):
    t, n = x.shape[0], w.shape[1]
    tn = min(tn, n)
    assert t % tm == 0 and n % tn == 0
    return pl.pallas_call(
        _norm_matmul_body,
        grid=(t // tm, n // tn),
        in_specs=[pl.BlockSpec((tm, D_MODEL), lambda i, j: (i, 0)),
                  pl.BlockSpec((1, D_MODEL), lambda i, j: (0, 0)),
                  pl.BlockSpec((D_MODEL, tn), lambda i, j: (0, j))],
        out_specs=pl.BlockSpec((tm, tn), lambda i, j: (i, j)),
        out_shape=jax.ShapeDtypeStruct((t, n), out_dtype),
        scratch_shapes=[pltpu.VMEM((tm, D_MODEL), BF16)],
        compiler_params=_params(("parallel", "arbitrary")),
        name="norm_matmul",
    )(x, g, w)


FOURIER_L1 = MXU_DIM


def _fourier_tables(l):
    l1 = FOURIER_L1
    l2 = l // l1
    g = MXU_DIM // l2
    w = FOURIER_GROUP_WIDTH
    c = np.arange(w)
    ang = 2.0 * np.pi * np.outer(c, c) / w
    cs_w = np.concatenate([np.cos(ang), np.sin(ang)], axis=1) / math.sqrt(w)
    n1 = np.arange(l1)
    ang1 = 2.0 * np.pi * np.outer(n1, n1) / l1
    c1, s1 = np.cos(ang1), np.sin(ang1)
    k1 = np.arange(l1).reshape(l1 // g, 1, g, 1, 1)
    k2 = np.arange(l2).reshape(1, l2, 1, 1, 1)
    n2 = np.arange(l2).reshape(1, 1, 1, 1, l2)
    theta = 2.0 * np.pi * (n2 * k2 / l2 + n2 * k1 / l)
    eye = np.eye(g).reshape(1, 1, g, g, 1)
    gc = (np.cos(theta) * eye).reshape(l1 // g, l2 * g, g * l2)
    gs = (np.sin(theta) * eye).reshape(l1 // g, l2 * g, g * l2)
    g2 = np.concatenate([gc, -gs], axis=2) / math.sqrt(l)
    as_bf16 = lambda a: jnp.asarray(a, dtype=F32).astype(BF16)
    return as_bf16(cs_w), as_bf16(c1), as_bf16(s1), as_bf16(g2)


def _fourier1_body(u_ref, csw_ref, c1_ref, s1_ref, zre_ref, zim_ref, *, slabs):
    w = FOURIER_GROUP_WIDTH
    for s in range(slabs):
        cols = slice(s * w, (s + 1) * w)
        ab = jnp.dot(u_ref[0, :, cols], csw_ref[...], preferred_element_type=F32).astype(BF16)
        p1 = jnp.dot(c1_ref[...], ab, preferred_element_type=F32)
        p2 = jnp.dot(s1_ref[...], ab, preferred_element_type=F32)
        zre_ref[0, :, cols] = (p1[:, :w] - p2[:, w:]).astype(BF16)
        zim_ref[0, :, cols] = (p1[:, w:] + p2[:, :w]).astype(BF16)


def _fourier2_body(zre_ref, zim_ref, g2_ref, o_ref):
    z = jnp.concatenate([zre_ref[0], zim_ref[0]], axis=0)
    res = jnp.dot(g2_ref[0], z, preferred_element_type=F32)
    o_ref[0] = res.reshape(o_ref.shape[1:])


def _fourier_branch(u, *, tcol=2048, ct=1024):
    b, l, c = u.shape
    l1 = FOURIER_L1
    l2 = l // l1
    g = MXU_DIM // l2
    assert l % l1 == 0 and MXU_DIM % l2 == 0 and g % 8 == 0 and (l2 * c) % tcol == 0
    csw, c1, s1, g2 = _fourier_tables(l)
    u1 = u.reshape(b, l1, l2 * c)
    zshape = jax.ShapeDtypeStruct((b, l1, l2 * c), BF16)
    zre, zim = pl.pallas_call(
        functools.partial(_fourier1_body, slabs=tcol // FOURIER_GROUP_WIDTH),
        grid=(b, l2 * c // tcol),
        in_specs=[pl.BlockSpec((1, l1, tcol), lambda i, j: (i, 0, j)),
                  _resident(csw.shape), _resident(c1.shape), _resident(s1.shape)],
        out_specs=[pl.BlockSpec((1, l1, tcol), lambda i, j: (i, 0, j))] * 2,
        out_shape=[zshape, zshape],
        compiler_params=_params(("parallel", "parallel")),
        name="fourier_stage1",
    )(u1, csw, c1, s1)
    zre = zre.reshape(b, l, c)
    zim = zim.reshape(b, l, c)
    y = pl.pallas_call(
        _fourier2_body,
        grid=(b, l1 // g, c // ct),
        in_specs=[pl.BlockSpec((1, MXU_DIM, ct), lambda i, t, j: (i, t, j)),
                  pl.BlockSpec((1, MXU_DIM, ct), lambda i, t, j: (i, t, j)),
                  pl.BlockSpec((1, MXU_DIM, 2 * MXU_DIM), lambda i, t, j: (t, 0, 0))],
        out_specs=pl.BlockSpec((1, l2, g, ct), lambda i, t, j: (i, 0, t, j)),
        out_shape=jax.ShapeDtypeStruct((b, l2, l1, c), F32),
        compiler_params=_params(("parallel", "parallel", "parallel")),
        name="fourier_stage2",
    )(zre, zim, g2)
    return y.reshape(b, l, c)


CONV_HALO = BF16_SUBLANES


def _conv_body(prev_ref, cur_ref, next_ref, w_ref, b_ref, o_ref, ext_ref, *, tl):
    i = pl.program_id(1)
    last = pl.num_programs(1) - 1
    halo = CONV_HALO
    ext_ref[0:halo, :] = jnp.where(i > 0, prev_ref[0].astype(F32), 0.0)
    ext_ref[halo:halo + tl, :] = cur_ref[0].astype(F32)
    ext_ref[halo + tl:, :] = jnp.where(i < last, next_ref[0].astype(F32), 0.0)
    acc = b_ref[...]
    for k in range(CONV_WIDTH):
        acc = acc + w_ref[k:k + 1, :] * ext_ref[pl.ds(halo - CONV_PAD + k, tl), :]
    o_ref[0] = (acc * jax.nn.sigmoid(acc)).astype(BF16)


def _conv_silu(p_main, conv_w, conv_b, *, tl=512, tc=512):
    b, l, _ = p_main.shape
    halo = CONV_HALO
    assert l % tl == 0 and tl % halo == 0 and CONV_DIM % tc == 0 and P_XBC % tc == 0
    col0 = P_XBC // tc
    hb = tl // halo
    nhb = l // halo
    return pl.pallas_call(
        functools.partial(_conv_body, tl=tl),
        grid=(b, l // tl, CONV_DIM // tc),
        in_specs=[pl.BlockSpec((1, halo, tc), lambda n, i, j: (n, jnp.maximum(i * hb - 1, 0), col0 + j)),
                  pl.BlockSpec((1, tl, tc), lambda n, i, j: (n, i, col0 + j)),
                  pl.BlockSpec((1, halo, tc), lambda n, i, j: (n, jnp.minimum((i + 1) * hb, nhb - 1), col0 + j)),
                  pl.BlockSpec((CONV_WIDTH, tc), lambda n, i, j: (0, j)),
                  pl.BlockSpec((1, tc), lambda n, i, j: (0, j))],
        out_specs=pl.BlockSpec((1, tl, tc), lambda n, i, j: (n, i, j)),
        out_shape=jax.ShapeDtypeStruct((b, l, CONV_DIM), BF16),
        scratch_shapes=[pltpu.VMEM((tl + 2 * halo, tc), F32)],
        compiler_params=_params(("parallel", "parallel", "parallel")),
        name="conv_silu",
    )(p_main, p_main, p_main, conv_w, conv_b)


HEADS_PER_PAIR = LANES // SSM_HEAD_DIM
N_PAIRS = SSM_HEADS // HEADS_PER_PAIR
HEADS_PER_GROUP = SSM_HEADS // SSM_GROUPS
PAIRS_PER_GROUP = HEADS_PER_GROUP // HEADS_PER_PAIR


def _split3(a):
    hi = a.astype(BF16)
    r1 = a - hi.astype(F32)
    mid = r1.astype(BF16)
    lo = (r1 - mid.astype(F32)).astype(BF16)
    return hi, mid, lo


def _ssd_body(*refs, direction, final):
    if final:
        (x_ref, b_ref, c_ref, dt_ref, bias_ref, alog_ref, yf_ref, z_ref, dskip_ref, ng_ref,
         o_ref, s_ref, yacc_ref) = refs
    else:
        x_ref, b_ref, c_ref, dt_ref, bias_ref, alog_ref, o_ref, s_ref = refs
    q = CHUNK

    @pl.when(pl.program_id(1) == 0)
    def _():
        s_ref[...] = jnp.zeros_like(s_ref)

    row = lax.broadcasted_iota(jnp.int32, (q, q), 0)
    col = lax.broadcasted_iota(jnp.int32, (q, q), 1)
    causal = (col <= row) if direction == 0 else (col >= row)
    tmat = jnp.where(causal, 1.0, 0.0).astype(BF16)
    end = q - 1 if direction == 0 else 0
    first_head = lax.broadcasted_iota(jnp.int32, (q, LANES), 1) < SSM_HEAD_DIM

    dtv = jax.nn.softplus(dt_ref[0] + bias_ref[...])
    a = dtv * (-jnp.exp(alog_ref[...]))
    cum = sum(jnp.dot(tmat, piece, preferred_element_type=F32) for piece in _split3(a))
    cum_t = cum.T
    dt_t = dtv.T
    w_t = jnp.exp(jnp.broadcast_to(cum_t[:, end:end + 1], (LANES, q)) - cum_t) * dt_t

    for pr in range(N_PAIRS):
        grp = pr // PAIRS_PER_GROUP
        if pr % PAIRS_PER_GROUP == 0:
            gcols = slice(grp * D_STATE, (grp + 1) * D_STATE)
            bg = b_ref[0, :, gcols]
            cg = c_ref[0, :, gcols]
            cb = lax.dot_general(cg, bg, (((1,), (1,)), ((), ())), preferred_element_type=F32)
            bg_t = bg.astype(F32).T
            cg_f = cg.astype(F32)
        xp = x_ref[0, :, pr * LANES:(pr + 1) * LANES]
        s_in = s_ref[pr]
        rhs = jnp.concatenate([xp, s_in.astype(BF16)], axis=0)
        ys, ss = [], []
        for i in range(HEADS_PER_PAIR):
            hc = SSM_HEADS * direction + HEADS_PER_PAIR * pr + i
            colb = jnp.broadcast_to(cum[:, hc:hc + 1], (q, LANES))
            decay = jnp.where(causal, jnp.exp(colb - cum_t[hc:hc + 1, :]), 0.0)
            m = (cb * decay * dt_t[hc:hc + 1, :]).astype(BF16)
            ce = (cg_f * jnp.exp(colb)).astype(BF16)
            ys.append(jnp.dot(jnp.concatenate([m, ce], axis=1), rhs, preferred_element_type=F32))
            bw = (bg_t * w_t[hc:hc + 1, :]).astype(BF16)
            chunk_decay = jnp.exp(colb[end:end + 1, :])
            ss.append(s_in * chunk_decay + jnp.dot(bw, xp, preferred_element_type=F32))
        y_pair = jnp.where(first_head, ys[0], ys[1])
        s_ref[pr] = jnp.where(first_head, ss[0], ss[1])
        if final:
            yacc_ref[:, pr * LANES:(pr + 1) * LANES] = y_pair
        else:
            o_ref[0, :, pr * LANES:(pr + 1) * LANES] = y_pair.astype(o_ref.dtype)

    if final:
        y = yacc_ref[...] + yf_ref[0].astype(F32) + x_ref[0].astype(F32) * dskip_ref[...]
        z = z_ref[0].astype(F32)
        o_ref[0] = _rms(y * (z * jax.nn.sigmoid(z)), ng_ref[...]).astype(o_ref.dtype)


def _ssd_pass(xbc, dt_raw, bias, alog, direction, final_inputs=None):
    b, l, _ = xbc.shape
    nc = l // CHUNK
    assert l % CHUNK == 0
    chunk = (lambda c: c) if direction == 0 else (lambda c: nc - 1 - c)
    xcol = D_INNER // (SSM_GROUPS * D_STATE)
    in_specs = [pl.BlockSpec((1, CHUNK, D_INNER), lambda n, c: (n, chunk(c), 0)),
                pl.BlockSpec((1, CHUNK, SSM_GROUPS * D_STATE), lambda n, c: (n, chunk(c), xcol)),
                pl.BlockSpec((1, CHUNK, SSM_GROUPS * D_STATE), lambda n, c: (n, chunk(c), xcol + 1)),
                pl.BlockSpec((1, CHUNK, DT_PAD), lambda n, c: (n, chunk(c), 0)),
                pl.BlockSpec((1, DT_PAD), lambda n, c: (0, 0)),
                pl.BlockSpec((1, DT_PAD), lambda n, c: (0, 0))]
    args = [xbc, xbc, xbc, dt_raw, bias, alog]
    scratch = [pltpu.VMEM((N_PAIRS, D_STATE, LANES), F32)]
    final = final_inputs is not None
    if final:
        y_fwd, p_main, dskip, norm_g = final_inputs
        in_specs += [pl.BlockSpec((1, CHUNK, D_INNER), lambda n, c: (n, chunk(c), 0)),
                     pl.BlockSpec((1, CHUNK, D_INNER), lambda n, c: (n, chunk(c), P_Z // D_INNER)),
                     pl.BlockSpec((1, D_INNER), lambda n, c: (0, 0)),
                     pl.BlockSpec((1, D_INNER), lambda n, c: (0, 0))]
        args += [y_fwd, p_main, dskip, norm_g]
        scratch.append(pltpu.VMEM((CHUNK, D_INNER), F32))
    return pl.pallas_call(
        functools.partial(_ssd_body, direction=direction, final=final),
        grid=(b, nc),
        in_specs=in_specs,
        out_specs=pl.BlockSpec((1, CHUNK, D_INNER), lambda n, c: (n, chunk(c), 0)),
        out_shape=jax.ShapeDtypeStruct((b, l, D_INNER), BF16),
        scratch_shapes=scratch,
        compiler_params=_params(("parallel", "arbitrary")),
        name="ssd_bwd_norm" if final else "ssd_fwd",
    )(*args)


def _merge_body(x_ref, yf_ref, ym_ref, gf_ref, gm_ref, wf_ref, wm_ref, wo_ref, post_ref, o_ref):
    f = jnp.dot(yf_ref[...].astype(BF16), wf_ref[...], preferred_element_type=F32)
    m = jnp.dot(ym_ref[...], wm_ref[...], preferred_element_type=F32)
    merged = (jax.nn.sigmoid(gf_ref[...].astype(F32)) * f
              + jax.nn.sigmoid(gm_ref[...].astype(F32)) * m).astype(BF16)
    out = jnp.dot(merged, wo_ref[...], preferred_element_type=F32)
    o_ref[...] = x_ref[...] + _rms(out, post_ref[...])


def _merge_block(x, yf, ym, p_main, w_f, w_m, w_o, post_g, *, tm=512):
    t = x.shape[0]
    assert t % tm == 0
    gcol = P_GATE // D_MODEL
    return pl.pallas_call(
        _merge_body,
        grid=(t // tm,),
        in_specs=[pl.BlockSpec((tm, D_MODEL), lambda i: (i, 0)),
                  pl.BlockSpec((tm, D_FOURIER), lambda i: (i, 0)),
                  pl.BlockSpec((tm, D_INNER), lambda i: (i, 0)),
                  pl.BlockSpec((tm, D_MODEL), lambda i: (i, gcol)),
                  pl.BlockSpec((tm, D_MODEL), lambda i: (i, gcol + 1)),
                  _resident((D_FOURIER, D_MODEL)), _resident((D_INNER, D_MODEL)),
                  _resident((D_MODEL, D_MODEL)), _resident((1, D_MODEL))],
        out_specs=pl.BlockSpec((tm, D_MODEL), lambda i: (i, 0)),
        out_shape=jax.ShapeDtypeStruct((t, D_MODEL), F32),
        compiler_params=_params(("parallel",)),
        name="merge_out",
    )(x, yf, ym, p_main, p_main, w_f, w_m, w_o, post_g)


def _prepare(ffn1_pre_g, ffn1_post_g, ffn1_w_in, ffn1_w_out, mix_pre_g, mix_post_g, w_in, conv_w,
             conv_b, dt_bias, a_log, d_skip, ssm_norm_g, w_branch_f, w_branch_m, w_out,
             ffn2_pre_g, ffn2_post_g, ffn2_w_in, ffn2_w_out):
    row = lambda v: v.reshape(1, -1).astype(F32)
    off_f, off_z = D_FOURIER, D_FOURIER + D_INNER
    off_xbc = off_z + CONV_DIM
    off_dt = off_xbc + 2 * SSM_HEADS
    w_main = jnp.concatenate([w_in[:, off_f:off_xbc], w_in[:, off_dt:]], axis=1).astype(BF16)
    pad = DT_PAD - 2 * SSM_HEADS
    w_dt = jnp.pad(w_in[:, off_xbc:off_dt], ((0, 0), (0, pad))).astype(BF16)
    return dict(
        ffn1=(row(ffn1_pre_g), row(ffn1_post_g), ffn1_w_in.astype(BF16), ffn1_w_out.astype(BF16)),
        ffn2=(row(ffn2_pre_g), row(ffn2_post_g), ffn2_w_in.astype(BF16), ffn2_w_out.astype(BF16)),
        mix_pre=row(mix_pre_g), mix_post=row(mix_post_g),
        w_main=w_main, w_four=w_in[:, :off_f].astype(BF16), w_dt=w_dt,
        conv_w=conv_w.astype(F32), conv_b=row(conv_b),
        dt_bias=jnp.pad(row(dt_bias), ((0, 0), (0, pad))),
        a_log=jnp.pad(row(a_log), ((0, 0), (0, pad))),
        d_skip=row(jnp.repeat(d_skip, SSM_HEAD_DIM)), norm_g=row(ssm_norm_g),
        w_f=w_branch_f.astype(BF16), w_m=w_branch_m.astype(BF16), w_o=w_out.astype(BF16))


def _encoder_layer(x, p):
    b, l, d = x.shape
    t = b * l
    x = _ffn_block(x.reshape(t, d), *p["ffn1"])
    p_main = _norm_matmul(x, p["mix_pre"], p["w_main"], BF16).reshape(b, l, P_MAIN)
    u_f = _norm_matmul(x, p["mix_pre"], p["w_four"], BF16).reshape(b, l, D_FOURIER)
    dt_raw = _norm_matmul(x, p["mix_pre"], p["w_dt"], F32).reshape(b, l, DT_PAD)
    yf = _fourier_branch(u_f)
    xbc = _conv_silu(p_main, p["conv_w"], p["conv_b"])
    y_fwd = _ssd_pass(xbc, dt_raw, p["dt_bias"], p["a_log"], 0)
    ym = _ssd_pass(xbc, dt_raw, p["dt_bias"], p["a_log"], 1,
                   final_inputs=(y_fwd, p_main, p["d_skip"], p["norm_g"]))
    x = _merge_block(x, yf.reshape(t, D_FOURIER), ym.reshape(t, D_INNER), p_main.reshape(t, P_MAIN),
                     p["w_f"], p["w_m"], p["w_o"], p["mix_post"])
    x = _ffn_block(x, *p["ffn2"])
    return x.reshape(b, l, d)


def kernel(x_prompt, x_sample, ffn1_pre_g, ffn1_post_g, ffn1_w_in, ffn1_w_out, mix_pre_g, mix_post_g, w_in, conv_w, conv_b, dt_bias, a_log, d_skip, ssm_norm_g, w_branch_f, w_branch_m, w_out, ffn2_pre_g, ffn2_post_g, ffn2_w_in, ffn2_w_out):
    layers = (ffn1_pre_g, ffn1_post_g, ffn1_w_in, ffn1_w_out, mix_pre_g, mix_post_g, w_in, conv_w,
              conv_b, dt_bias, a_log, d_skip, ssm_norm_g, w_branch_f, w_branch_m, w_out,
              ffn2_pre_g, ffn2_post_g, ffn2_w_in, ffn2_w_out)
    y_prompt, y_sample = x_prompt, x_sample
    for layer in range(ffn1_pre_g.shape[0]):
        p = _prepare(*(a[layer] for a in layers))
        y_prompt = _encoder_layer(y_prompt, p)
        y_sample = _encoder_layer(y_sample, p)
    return (y_prompt, y_sample)
```

```python
import functools
import math

import numpy as np
import jax
import jax.numpy as jnp
from jax import lax
from jax.experimental import pallas as pl
from jax.experimental.pallas import tpu as pltpu

F32 = jnp.float32
BF16 = jnp.bfloat16

D_MODEL = 1024
D_FF = 2816
FOURIER_GROUP_WIDTH = 256
D_FOURIER = 1024
SSM_HEADS = 32
SSM_HEAD_DIM = 64
D_INNER = SSM_HEADS * SSM_HEAD_DIM
SSM_GROUPS = 8
D_STATE = 128
CONV_WIDTH = 5
CONV_PAD = CONV_WIDTH // 2
CHUNK = 128
CONV_DIM = D_INNER + 2 * SSM_GROUPS * D_STATE
EPS = 1e-6

LANES = 128
MXU_DIM = 256
BF16_SUBLANES = 16
VMEM_LIMIT = 56 * 1024 * 1024

P_Z = 0
P_XBC = D_INNER
P_GATE = D_INNER + CONV_DIM
P_MAIN = P_GATE + 2 * D_MODEL
DT_PAD = LANES


def _params(sem):
    return pltpu.CompilerParams(dimension_semantics=sem, vmem_limit_bytes=VMEM_LIMIT)


def _rms(x, g):
    return x * lax.rsqrt(jnp.mean(x * x, axis=-1, keepdims=True) + EPS) * g


def _resident(shape):
    nd = len(shape)
    return pl.BlockSpec(shape, lambda *_: (0,) * nd, pipeline_mode=pl.Buffered(1))


def _ffn_body(x_ref, pre_ref, post_ref, win_ref, wout_ref, o_ref, h_ref, *, tf):
    x = x_ref[...]
    h_ref[...] = _rms(x, pre_ref[...]).astype(BF16)
    acc = None
    for c in range(D_FF // tf):
        h = h_ref[...]
        gate = jnp.dot(h, win_ref[:, c * tf:(c + 1) * tf], preferred_element_type=F32)
        up = jnp.dot(h, win_ref[:, D_FF + c * tf:D_FF + (c + 1) * tf], preferred_element_type=F32)
        act = (gate * jax.nn.sigmoid(gate) * up).astype(BF16)
        part = jnp.dot(act, wout_ref[c * tf:(c + 1) * tf, :], preferred_element_type=F32)
        acc = part if acc is None else acc + part
    o_ref[...] = x + 0.5 * _rms(acc, post_ref[...])


def _ffn_block(x, pre_g, post_g, w_in, w_out, *, tm=512, tf=256):
    t = x.shape[0]
    assert t % tm == 0 and D_FF % tf == 0
    return pl.pallas_call(
        functools.partial(_ffn_body, tf=tf),
        grid=(t // tm,),
        in_specs=[pl.BlockSpec((tm, D_MODEL), lambda i: (i, 0)),
                  _resident((1, D_MODEL)), _resident((1, D_MODEL)),
                  _resident((D_MODEL, 2 * D_FF)), _resident((D_FF, D_MODEL))],
        out_specs=pl.BlockSpec((tm, D_MODEL), lambda i: (i, 0)),
        out_shape=jax.ShapeDtypeStruct((t, D_MODEL), F32),
        scratch_shapes=[pltpu.VMEM((tm, D_MODEL), BF16)],
        compiler_params=_params(("parallel",)),
        name="ffn_block",
    )(x, pre_g, post_g, w_in, w_out)


def _norm_matmul_body(x_ref, g_ref, w_ref, o_ref, h_ref):
    @pl.when(pl.program_id(1) == 0)
    def _():
        h_ref[...] = _rms(x_ref[...], g_ref[...]).astype(BF16)

    o_ref[...] = jnp.dot(h_ref[...], w_ref[...], preferred_element_type=F32).astype(o_ref.dtype)


def _norm_matmul(x, g, w, out_dtype, *, tm=1024, tn=1024):
    t, n = x.shape[0], w.shape[1]
    tn = min(tn, n)
    assert t % tm == 0 and n % tn == 0
    return pl.pallas_call(
        _norm_matmul_body,
        grid=(t // tm, n // tn),
        in_specs=[pl.BlockSpec((tm, D_MODEL), lambda i, j: (i, 0)),
                  pl.BlockSpec((1, D_MODEL), lambda i, j: (0, 0)),
                  pl.BlockSpec((D_MODEL, tn), lambda i, j: (0, j))],
        out_specs=pl.BlockSpec((tm, tn), lambda i, j: (i, j)),
        out_shape=jax.ShapeDtypeStruct((t, n), out_dtype),
        scratch_shapes=[pltpu.VMEM((tm, D_MODEL), BF16)],
        compiler_params=_params(("parallel", "arbitrary")),
        name="norm_matmul",
    )(x, g, w)


FOURIER_L1 = MXU_DIM
SUBLANES = 8


def _fourier_tables(l):
    l1 = FOURIER_L1
    l2 = l // l1
    g = MXU_DIM // l2
    w = FOURIER_GROUP_WIDTH
    c = np.arange(w)
    ang = 2.0 * np.pi * np.outer(c, c) / w
    cs_w = np.concatenate([np.cos(ang), -np.sin(ang)], axis=0) / math.sqrt(w)
    n1 = np.arange(l1)
    ang1 = 2.0 * np.pi * np.outer(n1, n1) / l1
    cs1 = np.concatenate([np.cos(ang1), np.sin(ang1)], axis=0)
    k1 = np.arange(l1).reshape(l1 // g, 1, g, 1, 1)
    k2 = np.arange(l2).reshape(1, l2, 1, 1, 1)
    n2 = np.arange(l2).reshape(1, 1, 1, 1, l2)
    theta = 2.0 * np.pi * (n2 * k2 / l2 + n2 * k1 / l)
    eye = np.eye(g).reshape(1, 1, g, g, 1)

    def tiled_cols(m):
        m = m.reshape(l1 // g, l2 * g, g, l2 // SUBLANES, SUBLANES)
        return m.transpose(0, 1, 3, 2, 4).reshape(l1 // g, l2 * g, g * l2)

    gc = tiled_cols(np.cos(theta) * eye)
    gs = tiled_cols(np.sin(theta) * eye)
    g2 = np.concatenate([np.concatenate([gc, -gs], axis=2),
                         np.concatenate([gs, gc], axis=2)], axis=1) / math.sqrt(l)
    as_bf16 = lambda a: jnp.asarray(a, dtype=F32).astype(BF16)
    return as_bf16(cs_w), as_bf16(cs1), as_bf16(g2)


def _fourier_proj_body(x_ref, g_ref, w_ref, o_ref, *, l2):
    h = _rms(x_ref[...], g_ref[...]).astype(BF16)
    u = jnp.dot(h, w_ref[...], preferred_element_type=F32)
    tm, tn = u.shape
    u = u.reshape(tm // l2, l2 // SUBLANES, SUBLANES, tn)
    for t in range(l2 // SUBLANES):
        o_ref[0, t] = u[:, t].reshape(tm // l2 * SUBLANES, tn)


def _fourier_proj(x, g, w, b, l, *, tm=1024):
    t, c = x.shape[0], w.shape[1]
    l1 = FOURIER_L1
    l2 = l // l1
    per_seq = l // tm
    assert t == b * l and l % tm == 0 and tm % l2 == 0 and l2 % SUBLANES == 0
    return pl.pallas_call(
        functools.partial(_fourier_proj_body, l2=l2),
        grid=(t // tm,),
        in_specs=[pl.BlockSpec((tm, D_MODEL), lambda i: (i, 0)),
                  _resident((1, D_MODEL)), _resident((D_MODEL, c))],
        out_specs=pl.BlockSpec((1, l2 // SUBLANES, tm // l2 * SUBLANES, c),
                               lambda i: (i // per_seq, 0, i % per_seq, 0)),
        out_shape=jax.ShapeDtypeStruct((b, l2 // SUBLANES, l1 * SUBLANES, c), F32),
        compiler_params=_params(("parallel",)),
        name="fourier_proj",
    )(x, g, w)


def _fourier1_body(u_ref, cs1_ref, zre_ref, zim_ref):
    l1 = FOURIER_L1
    for j in range(SUBLANES):
        pick = pl.ds(j, l1, stride=SUBLANES)
        z = jnp.dot(cs1_ref[...], u_ref[pick, :].astype(BF16), preferred_element_type=F32)
        zre_ref[pick, :] = z[:l1]
        zim_ref[pick, :] = z[l1:]


def _fourier2_body(zre_ref, zim_ref, g2_ref, csw_ref, o_ref):
    w = FOURIER_GROUP_WIDTH
    half = MXU_DIM
    ct = o_ref.shape[-1]
    z = jnp.concatenate([zre_ref[0].reshape(half, ct).astype(BF16),
                         zim_ref[0].reshape(half, ct).astype(BF16)], axis=0)
    v = jnp.dot(g2_ref[0], z, preferred_element_type=F32).astype(BF16)
    for s in range(z.shape[1] // w):
        cols = slice(s * w, (s + 1) * w)
        vv = jnp.concatenate([v[:half, cols], v[half:, cols]], axis=1)
        res = jnp.dot(vv, csw_ref[...], preferred_element_type=F32)
        o_ref[0, :, :, cols] = res.reshape(o_ref.shape[1], o_ref.shape[2], w)


def _fourier_branch(u, l, *, ct=1024):
    b, nt, rows, c = u.shape
    l1 = FOURIER_L1
    l2 = l // l1
    g = MXU_DIM // l2
    assert nt * SUBLANES == l2 and rows == l1 * SUBLANES and MXU_DIM % l2 == 0 and g % SUBLANES == 0
    csw, cs1, g2 = _fourier_tables(l)
    blk = pl.BlockSpec((None, None, rows, LANES), lambda i, j, k: (i, j, 0, k))
    zshape = jax.ShapeDtypeStruct(u.shape, F32)
    zre, zim = pl.pallas_call(
        _fourier1_body,
        grid=(b, nt, c // LANES),
        in_specs=[blk, _resident(cs1.shape)],
        out_specs=[blk, blk],
        out_shape=[zshape, zshape],
        compiler_params=_params(("parallel", "parallel", "parallel")),
        name="fourier_stage1",
    )(u, cs1)
    zblk = pl.BlockSpec((1, nt, g * SUBLANES, ct), lambda i, t, j: (i, 0, t, j))
    y = pl.pallas_call(
        _fourier2_body,
        grid=(b, l1 // g, c // ct),
        in_specs=[zblk, zblk,
                  pl.BlockSpec((1, 2 * MXU_DIM, 2 * MXU_DIM), lambda i, t, j: (t, 0, 0)),
                  _resident(csw.shape)],
        out_specs=pl.BlockSpec((1, l2, g, ct), lambda i, t, j: (i, 0, t, j)),
        out_shape=jax.ShapeDtypeStruct((b, l2, l1, c), F32),
        compiler_params=_params(("parallel", "parallel", "parallel")),
        name="fourier_stage2",
    )(zre, zim, g2, csw)
    return y.reshape(b, l, c)


CONV_HALO = BF16_SUBLANES


def _proj_main_body(xp_ref, x_ref, xn_ref, g_ref, w_ref, cw_ref, cb_ref, o_ref, h_ref, ext_ref,
                    *, tm, sub, conv_tiles, tiles_per_seq):
    i, j = pl.program_id(0), pl.program_id(1)
    halo = CONV_HALO

    @pl.when(j == 0)
    def _():
        h_ref[0:halo, :] = _rms(xp_ref[...], g_ref[...]).astype(BF16)
        h_ref[halo:halo + tm, :] = _rms(x_ref[...], g_ref[...]).astype(BF16)
        h_ref[halo + tm:, :] = _rms(xn_ref[...], g_ref[...]).astype(BF16)

    is_conv = jnp.logical_and(j >= conv_tiles[0], j < conv_tiles[1])

    @pl.when(jnp.logical_not(is_conv))
    def _():
        o_ref[...] = jnp.dot(h_ref[halo:halo + tm, :], w_ref[...],
                             preferred_element_type=F32).astype(o_ref.dtype)

    @pl.when(is_conv)
    def _():
        seq_first = (i % tiles_per_seq) == 0
        seq_last = (i % tiles_per_seq) == tiles_per_seq - 1
        for r in range(tm // sub):
            ext = ext_ref.at[r]
            ext[...] = jnp.dot(h_ref[r * sub:r * sub + sub + 2 * halo, :], w_ref[...],
                               preferred_element_type=F32)
            if r == 0:
                ext[0:halo, :] = jnp.where(seq_first, 0.0, ext[0:halo, :])
            if r == tm // sub - 1:
                ext[halo + sub:, :] = jnp.where(seq_last, 0.0, ext[halo + sub:, :])
            tiles = sub // SUBLANES
            win = ext[halo - SUBLANES:halo + sub + SUBLANES, :].reshape(tiles + 2, SUBLANES, -1)
            in_tile = lax.broadcasted_iota(jnp.int32, (1, SUBLANES, 1), 1)
            acc = cb_ref[...] + cw_ref[CONV_PAD:CONV_PAD + 1, :] * win[1:tiles + 1]
            for k in range(CONV_WIDTH):
                d = k - CONV_PAD
                if d != 0:
                    rot = pltpu.roll(win, (-d) % SUBLANES, axis=1)
                    if d > 0:
                        shifted = jnp.where(in_tile < SUBLANES - d, rot[1:tiles + 1], rot[2:tiles + 2])
                    else:
                        shifted = jnp.where(in_tile >= -d, rot[1:tiles + 1], rot[0:tiles])
                    acc = acc + cw_ref[k:k + 1, :] * shifted
            acc = acc.reshape(sub, -1)
            o_ref[r * sub:(r + 1) * sub, :] = (acc * jax.nn.sigmoid(acc)).astype(o_ref.dtype)


def _proj_main(x, g, w, conv_w, conv_b, l, *, tm=1024, tn=1024, sub=512):
    t, n = x.shape[0], w.shape[1]
    halo = CONV_HALO
    assert t % tm == 0 and l % tm == 0 and n % tn == 0 and tm % sub == 0 and tm % halo == 0
    assert P_XBC % tn == 0 and P_GATE % tn == 0
    conv_tiles = (P_XBC // tn, P_GATE // tn)
    hb = tm // halo
    nhb = t // halo
    conv_col = lambda j: jnp.clip(j - conv_tiles[0], 0, conv_tiles[1] - conv_tiles[0] - 1)
    return pl.pallas_call(
        functools.partial(_proj_main_body, tm=tm, sub=sub, conv_tiles=conv_tiles,
                          tiles_per_seq=l // tm),
        grid=(t // tm, n // tn),
        in_specs=[pl.BlockSpec((halo, D_MODEL), lambda i, j: (jnp.maximum(i * hb - 1, 0), 0)),
                  pl.BlockSpec((tm, D_MODEL), lambda i, j: (i, 0)),
                  pl.BlockSpec((halo, D_MODEL), lambda i, j: (jnp.minimum((i + 1) * hb, nhb - 1), 0)),
                  pl.BlockSpec((1, D_MODEL), lambda i, j: (0, 0)),
                  pl.BlockSpec((D_MODEL, tn), lambda i, j: (0, j)),
                  pl.BlockSpec((CONV_WIDTH, tn), lambda i, j: (0, conv_col(j))),
                  pl.BlockSpec((1, tn), lambda i, j: (0, conv_col(j)))],
        out_specs=pl.BlockSpec((tm, tn), lambda i, j: (i, j)),
        out_shape=jax.ShapeDtypeStruct((t, n), BF16),
        scratch_shapes=[pltpu.VMEM((tm + 2 * halo, D_MODEL), BF16),
                        pltpu.VMEM((tm // sub, sub + 2 * halo, tn), F32)],
        compiler_params=_params(("parallel", "arbitrary")),
        name="proj_main",
    )(x, x, x, g, w, conv_w, conv_b)


LOG2E = math.log2(math.e)

HEADS_PER_PAIR = LANES // SSM_HEAD_DIM
N_PAIRS = SSM_HEADS // HEADS_PER_PAIR
HEADS_PER_GROUP = SSM_HEADS // SSM_GROUPS
PAIRS_PER_GROUP = HEADS_PER_GROUP // HEADS_PER_PAIR


def _split3(a):
    hi = a.astype(BF16)
    r1 = a - hi.astype(F32)
    mid = r1.astype(BF16)
    lo = (r1 - mid.astype(F32)).astype(BF16)
    return hi, mid, lo


def _ssd_prep_body(dt_ref, bias_ref, alog_ref, cum_ref, rowb_ref, w_ref, *, chunks):
    q = CHUNK
    row = lax.broadcasted_iota(jnp.int32, (q, q), 0)
    col = lax.broadcasted_iota(jnp.int32, (q, q), 1)
    tri = jnp.concatenate([jnp.where(col <= row, 1.0, 0.0), jnp.where(col >= row, 1.0, 0.0)],
                          axis=0).astype(BF16)
    fwd_lane = lax.broadcasted_iota(jnp.int32, (q, LANES), 1) < SSM_HEADS
    neg_a = -jnp.exp(alog_ref[...]) * LOG2E
    for c in range(chunks):
        rows = slice(c * q, (c + 1) * q)
        dtv = jax.nn.softplus(dt_ref[0, rows, :] + bias_ref[...])
        a = dtv * neg_a
        both = sum(jnp.dot(tri, piece, preferred_element_type=F32) for piece in _split3(a))
        cum2 = jnp.where(fwd_lane, both[:q], both[q:])
        cend = jnp.where(fwd_lane[0:1], cum2[q - 1:q], cum2[0:1])
        cum_ref[0, rows, :] = cum2
        rowb_ref[0, rows, :] = (cum2 - jnp.log2(dtv)).T
        w_ref[0, rows, :] = (jnp.exp2(cend - cum2) * dtv).T


def _ssd_prep(dt_raw, bias, alog, *, chunks=4):
    b, l, _ = dt_raw.shape
    rows = chunks * CHUNK
    assert l % rows == 0
    spec = pl.BlockSpec((1, rows, DT_PAD), lambda n, c: (n, c, 0))
    vec = pl.BlockSpec((1, DT_PAD), lambda n, c: (0, 0))
    shape = jax.ShapeDtypeStruct((b, l, DT_PAD), F32)
    return pl.pallas_call(
        functools.partial(_ssd_prep_body, chunks=chunks),
        grid=(b, l // rows),
        in_specs=[spec, vec, vec],
        out_specs=[spec, spec, spec],
        out_shape=[shape, shape, shape],
        compiler_params=_params(("parallel", "parallel")),
        name="ssd_prep",
    )(dt_raw, bias, alog)


def _ssd_body(*refs, direction, final):
    if final:
        (x_ref, b_ref, c_ref, cum_ref, rowb_ref, w_ref, yf_ref, z_ref, dskip_ref, ng_ref,
         o_ref, s_ref, yacc_ref) = refs
    else:
        x_ref, b_ref, c_ref, cum_ref, rowb_ref, w_ref, o_ref, s_ref = refs
    q = CHUNK

    @pl.when(pl.program_id(1) == 0)
    def _():
        s_ref[...] = jnp.zeros_like(s_ref)

    row = lax.broadcasted_iota(jnp.int32, (q, q), 0)
    col = lax.broadcasted_iota(jnp.int32, (q, q), 1)
    causal = (col <= row) if direction == 0 else (col >= row)
    end = q - 1 if direction == 0 else 0
    first_head = lax.broadcasted_iota(jnp.int32, (1, LANES), 1) < SSM_HEAD_DIM
    keep_bf = [jnp.where(first_head, 1.0, 0.0).astype(BF16),
               jnp.where(first_head, 0.0, 1.0).astype(BF16)]

    for pr in range(N_PAIRS):
        grp = pr // PAIRS_PER_GROUP
        if pr % PAIRS_PER_GROUP == 0:
            gcols = slice(grp * D_STATE, (grp + 1) * D_STATE)
            bg = b_ref[0, :, gcols]
            cg = c_ref[0, :, gcols]
            cb = lax.dot_general(cg, bg, (((1,), (1,)), ((), ())), preferred_element_type=F32)
            bg_t = bg.astype(F32).T
            s_grp = jnp.concatenate([s_ref[pr + k].astype(BF16) for k in range(PAIRS_PER_GROUP)], axis=1)
            y_off = jnp.dot(cg, s_grp, preferred_element_type=F32)
        xp = x_ref[0, :, pr * LANES:(pr + 1) * LANES]
        ms, bws, xs, ecols, cds = [], [], [], [], []
        for i in range(HEADS_PER_PAIR):
            hc = SSM_HEADS * direction + HEADS_PER_PAIR * pr + i
            colb = jnp.broadcast_to(cum_ref[0, :, hc:hc + 1], (q, LANES))
            m = jnp.where(causal, jnp.exp2(colb - rowb_ref[0, hc:hc + 1, :]), 0.0) * cb
            ms.append(m.astype(BF16))
            xs.append(xp * keep_bf[i])
            bws.append((bg_t * w_ref[0, hc:hc + 1, :]).astype(BF16))
            ecols.append(jnp.exp2(colb))
            cds.append(ecols[i][end:end + 1, :])
        x_split = jnp.concatenate(xs, axis=0)
        y_diag = jnp.dot(jnp.concatenate(ms, axis=1), x_split, preferred_element_type=F32)
        k = pr % PAIRS_PER_GROUP
        y_pair = y_diag + y_off[:, k * LANES:(k + 1) * LANES] * jnp.where(first_head, ecols[0], ecols[1])
        s_new = jnp.dot(jnp.concatenate(bws, axis=1), x_split, preferred_element_type=F32)
        s_ref[pr] = s_ref[pr] * jnp.where(first_head, cds[0], cds[1]) + s_new
        if final:
            yacc_ref[:, pr * LANES:(pr + 1) * LANES] = y_pair
        else:
            o_ref[0, :, pr * LANES:(pr + 1) * LANES] = y_pair.astype(o_ref.dtype)

    if final:
        y = yacc_ref[...] + yf_ref[0].astype(F32) + x_ref[0].astype(F32) * dskip_ref[...]
        z = z_ref[0].astype(F32)
        o_ref[0] = _rms(y * (z * jax.nn.sigmoid(z)), ng_ref[...]).astype(o_ref.dtype)


def _ssd_pass(p_main, prep, direction, final_inputs=None):
    b, l, _ = p_main.shape
    nc = l // CHUNK
    assert l % CHUNK == 0 and P_XBC % D_INNER == 0
    chunk = (lambda c: c) if direction == 0 else (lambda c: nc - 1 - c)
    bc_width = SSM_GROUPS * D_STATE
    xcol = P_XBC // D_INNER
    bcol = (P_XBC + D_INNER) // bc_width
    small = pl.BlockSpec((1, CHUNK, DT_PAD), lambda n, c: (n, chunk(c), 0))
    in_specs = [pl.BlockSpec((1, CHUNK, D_INNER), lambda n, c: (n, chunk(c), xcol)),
                pl.BlockSpec((1, CHUNK, bc_width), lambda n, c: (n, chunk(c), bcol)),
                pl.BlockSpec((1, CHUNK, bc_width), lambda n, c: (n, chunk(c), bcol + 1)),
                small, small, small]
    args = [p_main, p_main, p_main, *prep]
    scratch = [pltpu.VMEM((N_PAIRS, D_STATE, LANES), F32)]
    final = final_inputs is not None
    if final:
        y_fwd, dskip, norm_g = final_inputs
        in_specs += [pl.BlockSpec((1, CHUNK, D_INNER), lambda n, c: (n, chunk(c), 0)),
                     pl.BlockSpec((1, CHUNK, D_INNER), lambda n, c: (n, chunk(c), P_Z // D_INNER)),
                     pl.BlockSpec((1, D_INNER), lambda n, c: (0, 0)),
                     pl.BlockSpec((1, D_INNER), lambda n, c: (0, 0))]
        args += [y_fwd, p_main, dskip, norm_g]
        scratch.append(pltpu.VMEM((CHUNK, D_INNER), F32))
    return pl.pallas_call(
        functools.partial(_ssd_body, direction=direction, final=final),
        grid=(b, nc),
        in_specs=in_specs,
        out_specs=pl.BlockSpec((1, CHUNK, D_INNER), lambda n, c: (n, chunk(c), 0)),
        out_shape=jax.ShapeDtypeStruct((b, l, D_INNER), BF16),
        scratch_shapes=scratch,
        compiler_params=_params(("parallel", "arbitrary")),
        name="ssd_bwd_norm" if final else "ssd_fwd",
    )(*args)


def _merge_body(x_ref, yf_ref, ym_ref, gf_ref, gm_ref, wf_ref, wm_ref, wo_ref, post_ref, o_ref):
    f = jnp.dot(yf_ref[...].astype(BF16), wf_ref[...], preferred_element_type=F32)
    m = jnp.dot(ym_ref[...], wm_ref[...], preferred_element_type=F32)
    merged = (jax.nn.sigmoid(gf_ref[...].astype(F32)) * f
              + jax.nn.sigmoid(gm_ref[...].astype(F32)) * m).astype(BF16)
    out = jnp.dot(merged, wo_ref[...], preferred_element_type=F32)
    o_ref[...] = x_ref[...] + _rms(out, post_ref[...])


def _merge_block(x, yf, ym, p_main, w_f, w_m, w_o, post_g, *, tm=512):
    t = x.shape[0]
    assert t % tm == 0
    gcol = P_GATE // D_MODEL
    return pl.pallas_call(
        _merge_body,
        grid=(t // tm,),
        in_specs=[pl.BlockSpec((tm, D_MODEL), lambda i: (i, 0)),
                  pl.BlockSpec((tm, D_FOURIER), lambda i: (i, 0)),
                  pl.BlockSpec((tm, D_INNER), lambda i: (i, 0)),
                  pl.BlockSpec((tm, D_MODEL), lambda i: (i, gcol)),
                  pl.BlockSpec((tm, D_MODEL), lambda i: (i, gcol + 1)),
                  _resident((D_FOURIER, D_MODEL)), _resident((D_INNER, D_MODEL)),
                  _resident((D_MODEL, D_MODEL)), _resident((1, D_MODEL))],
        out_specs=pl.BlockSpec((tm, D_MODEL), lambda i: (i, 0)),
        out_shape=jax.ShapeDtypeStruct((t, D_MODEL), F32),
        compiler_params=_params(("parallel",)),
        name="merge_out",
    )(x, yf, ym, p_main, p_main, w_f, w_m, w_o, post_g)


def _prepare(ffn1_pre_g, ffn1_post_g, ffn1_w_in, ffn1_w_out, mix_pre_g, mix_post_g, w_in, conv_w,
             conv_b, dt_bias, a_log, d_skip, ssm_norm_g, w_branch_f, w_branch_m, w_out,
             ffn2_pre_g, ffn2_post_g, ffn2_w_in, ffn2_w_out):
    row = lambda v: v.reshape(1, -1).astype(F32)
    off_f, off_z = D_FOURIER, D_FOURIER + D_INNER
    off_xbc = off_z + CONV_DIM
    off_dt = off_xbc + 2 * SSM_HEADS
    w_main = jnp.concatenate([w_in[:, off_f:off_xbc], w_in[:, off_dt:]], axis=1).astype(BF16)
    pad = DT_PAD - 2 * SSM_HEADS
    w_dt = jnp.pad(w_in[:, off_xbc:off_dt], ((0, 0), (0, pad))).astype(BF16)
    return dict(
        ffn1=(row(ffn1_pre_g), row(ffn1_post_g), ffn1_w_in.astype(BF16), ffn1_w_out.astype(BF16)),
        ffn2=(row(ffn2_pre_g), row(ffn2_post_g), ffn2_w_in.astype(BF16), ffn2_w_out.astype(BF16)),
        mix_pre=row(mix_pre_g), mix_post=row(mix_post_g),
        w_main=w_main, w_four=w_in[:, :off_f].astype(BF16), w_dt=w_dt,
        conv_w=conv_w.astype(F32), conv_b=row(conv_b),
        dt_bias=jnp.pad(row(dt_bias), ((0, 0), (0, pad))),
        a_log=jnp.pad(row(a_log), ((0, 0), (0, pad))),
        d_skip=row(jnp.repeat(d_skip, SSM_HEAD_DIM)), norm_g=row(ssm_norm_g),
        w_f=w_branch_f.astype(BF16), w_m=w_branch_m.astype(BF16), w_o=w_out.astype(BF16))


def _encoder_layer(x, p):
    b, l, d = x.shape
    t = b * l
    x = _ffn_block(x.reshape(t, d), *p["ffn1"])
    p_main = _proj_main(x, p["mix_pre"], p["w_main"], p["conv_w"], p["conv_b"], l).reshape(b, l, P_MAIN)
    u_f = _fourier_proj(x, p["mix_pre"], p["w_four"], b, l)
    dt_raw = _norm_matmul(x, p["mix_pre"], p["w_dt"], F32).reshape(b, l, DT_PAD)
    yf = _fourier_branch(u_f, l)
    prep = _ssd_prep(dt_raw, p["dt_bias"], p["a_log"])
    y_fwd = _ssd_pass(p_main, prep, 0)
    ym = _ssd_pass(p_main, prep, 1, final_inputs=(y_fwd, p["d_skip"], p["norm_g"]))
    x = _merge_block(x, yf.reshape(t, D_FOURIER), ym.reshape(t, D_INNER), p_main.reshape(t, P_MAIN),
                     p["w_f"], p["w_m"], p["w_o"], p["mix_post"])
    x = _ffn_block(x, *p["ffn2"])
    return x.reshape(b, l, d)


def kernel(x_prompt, x_sample, ffn1_pre_g, ffn1_post_g, ffn1_w_in, ffn1_w_out, mix_pre_g, mix_post_g, w_in, conv_w, conv_b, dt_bias, a_log, d_skip, ssm_norm_g, w_branch_f, w_branch_m, w_out, ffn2_pre_g, ffn2_post_g, ffn2_w_in, ffn2_w_out):
    layers = (ffn1_pre_g, ffn1_post_g, ffn1_w_in, ffn1_w_out, mix_pre_g, mix_post_g, w_in, conv_w,
              conv_b, dt_bias, a_log, d_skip, ssm_norm_g, w_branch_f, w_branch_m, w_out,
              ffn2_pre_g, ffn2_post_g, ffn2_w_in, ffn2_w_out)
    y_prompt, y_sample = x_prompt, x_sample
    for layer in range(ffn1_pre_g.shape[0]):
        p = _prepare(*(a[layer] for a in layers))
        y_prompt = _encoder_layer(y_prompt, p)
        y_sample = _encoder_layer(y_sample, p)
    return (y_prompt, y_sample)
```

```python
import functools
import math

import numpy as np
import jax
import jax.numpy as jnp
from jax import lax
from jax.experimental import pallas as pl
from jax.experimental.pallas import tpu as pltpu

F32 = jnp.float32
BF16 = jnp.bfloat16

D_MODEL = 1024
D_FF = 2816
FOURIER_GROUP_WIDTH = 256
D_FOURIER = 1024
SSM_HEADS = 32
SSM_HEAD_DIM = 64
D_INNER = SSM_HEADS * SSM_HEAD_DIM
SSM_GROUPS = 8
D_STATE = 128
CONV_WIDTH = 5
CONV_PAD = CONV_WIDTH // 2
CHUNK = 128
CONV_DIM = D_INNER + 2 * SSM_GROUPS * D_STATE
EPS = 1e-6

LANES = 128
MXU_DIM = 256
BF16_SUBLANES = 16
VMEM_LIMIT = 56 * 1024 * 1024

P_TILE = 1024
P_PLAIN = (0, 2, 4, 6)
P_CONV = (1, 3, 5, 7)
P_MAIN = (len(P_PLAIN) + len(P_CONV)) * P_TILE
DT_PAD = LANES


def _params(sem):
    return pltpu.CompilerParams(dimension_semantics=sem, vmem_limit_bytes=VMEM_LIMIT)


def _rms(x, g):
    return x * lax.rsqrt(jnp.mean(x * x, axis=-1, keepdims=True) + EPS) * g


def _resident(shape):
    nd = len(shape)
    return pl.BlockSpec(shape, lambda *_: (0,) * nd, pipeline_mode=pl.Buffered(1))


def _ffn_body(x_ref, pre_ref, post_ref, win_ref, wout_ref, o_ref, h_ref, *, tf):
    x = x_ref[...]
    h_ref[...] = _rms(x, pre_ref[...]).astype(BF16)
    acc = None
    for c in range(D_FF // tf):
        h = h_ref[...]
        gate = jnp.dot(h, win_ref[:, c * tf:(c + 1) * tf], preferred_element_type=F32)
        up = jnp.dot(h, win_ref[:, D_FF + c * tf:D_FF + (c + 1) * tf], preferred_element_type=F32)
        act = (gate * jax.nn.sigmoid(gate) * up).astype(BF16)
        part = jnp.dot(act, wout_ref[c * tf:(c + 1) * tf, :], preferred_element_type=F32)
        acc = part if acc is None else acc + part
    o_ref[...] = x + 0.5 * _rms(acc, post_ref[...])


def _ffn_block(x, pre_g, post_g, w_in, w_out, *, tm=512, tf=256):
    t = x.shape[0]
    assert t % tm == 0 and D_FF % tf == 0
    return pl.pallas_call(
        functools.partial(_ffn_body, tf=tf),
        grid=(t // tm,),
        in_specs=[pl.BlockSpec((tm, D_MODEL), lambda i: (i, 0)),
                  _resident((1, D_MODEL)), _resident((1, D_MODEL)),
                  _resident((D_MODEL, 2 * D_FF)), _resident((D_FF, D_MODEL))],
        out_specs=pl.BlockSpec((tm, D_MODEL), lambda i: (i, 0)),
        out_shape=jax.ShapeDtypeStruct((t, D_MODEL), F32),
        scratch_shapes=[pltpu.VMEM((tm, D_MODEL), BF16)],
        compiler_params=_params(("parallel",)),
        name="ffn_block",
    )(x, pre_g, post_g, w_in, w_out)


FOURIER_L1 = MXU_DIM
SUBLANES = 8


def _fourier_tables(l):
    l1 = FOURIER_L1
    l2 = l // l1
    g = MXU_DIM // l2
    w = FOURIER_GROUP_WIDTH
    c = np.arange(w)
    ang = 2.0 * np.pi * np.outer(c, c) / w
    cs_w = np.concatenate([np.cos(ang), -np.sin(ang)], axis=0) / math.sqrt(w)
    n1 = np.arange(l1)
    ang1 = 2.0 * np.pi * np.outer(n1, n1) / l1
    cs1 = np.concatenate([np.cos(ang1), np.sin(ang1)], axis=0)
    k1 = np.arange(l1).reshape(l1 // g, 1, g, 1, 1)
    k2 = np.arange(l2).reshape(1, l2, 1, 1, 1)
    n2 = np.arange(l2).reshape(1, 1, 1, 1, l2)
    theta = 2.0 * np.pi * (n2 * k2 / l2 + n2 * k1 / l)
    eye = np.eye(g).reshape(1, 1, g, g, 1)

    def tiled_cols(m):
        m = m.reshape(l1 // g, l2 * g, g, l2 // SUBLANES, SUBLANES)
        return m.transpose(0, 1, 3, 2, 4).reshape(l1 // g, l2 * g, g * l2)

    gc = tiled_cols(np.cos(theta) * eye)
    gs = tiled_cols(np.sin(theta) * eye)
    g2 = np.concatenate([np.concatenate([gc, -gs], axis=2),
                         np.concatenate([gs, gc], axis=2)], axis=1) / math.sqrt(l)
    as_bf16 = lambda a: jnp.asarray(a, dtype=F32).astype(BF16)
    return as_bf16(cs_w), as_bf16(cs1), as_bf16(g2)


def _fourier_proj_body(x_ref, g_ref, w_ref, wdt_ref, o_ref, dt_ref, *, l2):
    h = _rms(x_ref[...], g_ref[...]).astype(BF16)
    dt_ref[...] = jnp.dot(h, wdt_ref[...], preferred_element_type=F32)
    u = jnp.dot(h, w_ref[...], preferred_element_type=F32)
    tm, tn = u.shape
    u = u.reshape(tm // l2, l2 // SUBLANES, SUBLANES, tn)
    for t in range(l2 // SUBLANES):
        ut = u[:, t].reshape(tm // l2 * SUBLANES, tn)
        for s in range(tn // LANES):
            o_ref[0, t, s] = ut[:, s * LANES:(s + 1) * LANES]


def _fourier_proj(x, g, w, w_dt, b, l, *, tm=1024):
    t, c = x.shape[0], w.shape[1]
    l1 = FOURIER_L1
    l2 = l // l1
    per_seq = l // tm
    assert t == b * l and l % tm == 0 and tm % l2 == 0 and l2 % SUBLANES == 0 and c % LANES == 0
    return pl.pallas_call(
        functools.partial(_fourier_proj_body, l2=l2),
        grid=(t // tm,),
        in_specs=[pl.BlockSpec((tm, D_MODEL), lambda i: (i, 0)),
                  _resident((1, D_MODEL)), _resident((D_MODEL, c)), _resident((D_MODEL, DT_PAD))],
        out_specs=[pl.BlockSpec((1, l2 // SUBLANES, c // LANES, tm // l2 * SUBLANES, LANES),
                                lambda i: (i // per_seq, 0, 0, i % per_seq, 0)),
                   pl.BlockSpec((tm, DT_PAD), lambda i: (i, 0))],
        out_shape=[jax.ShapeDtypeStruct((b, l2 // SUBLANES, c // LANES, l1 * SUBLANES, LANES), F32),
                   jax.ShapeDtypeStruct((t, DT_PAD), F32)],
        compiler_params=_params(("parallel",)),
        name="fourier_proj",
    )(x, g, w, w_dt)


def _pack_bf16_pair(hi, lo):
    bits = lambda v: lax.bitcast_convert_type(v.astype(BF16).astype(F32), jnp.uint32)
    return bits(hi) | (bits(lo) >> 16)


def _unpack_bf16_pair(word):
    hi = lax.bitcast_convert_type(word & jnp.uint32(0xFFFF0000), F32).astype(BF16)
    lo = lax.bitcast_convert_type(word << 16, F32).astype(BF16)
    return hi, lo


def _fourier1_body(u_ref, cs1_ref, z_ref):
    l1 = FOURIER_L1
    slabs = u_ref.shape[0]
    for j in range(SUBLANES):
        pick = pl.ds(j, l1, stride=SUBLANES)
        u = jnp.concatenate([u_ref.at[s][pick, :] for s in range(slabs)], axis=1).astype(BF16)
        z = jnp.dot(cs1_ref[...], u, preferred_element_type=F32)
        word = _pack_bf16_pair(z[:l1], z[l1:])
        for s in range(slabs):
            z_ref.at[s][pick, :] = word[:, s * LANES:(s + 1) * LANES]


def _fourier2_body(z_ref, g2_ref, csw_ref, o_ref):
    w = FOURIER_GROUP_WIDTH
    half = MXU_DIM
    slabs = z_ref.shape[2]
    parts = [_unpack_bf16_pair(z_ref[0, :, s].reshape(half, LANES)) for s in range(slabs)]
    z = jnp.concatenate([jnp.concatenate([p[0] for p in parts], axis=1),
                         jnp.concatenate([p[1] for p in parts], axis=1)], axis=0)
    v = jnp.dot(g2_ref[0], z, preferred_element_type=F32).astype(BF16)
    for s in range(z.shape[1] // w):
        cols = slice(s * w, (s + 1) * w)
        vv = jnp.concatenate([v[:half, cols], v[half:, cols]], axis=1)
        res = jnp.dot(vv, csw_ref[...], preferred_element_type=F32)
        o_ref[0, :, :, cols] = res.reshape(o_ref.shape[1], o_ref.shape[2], w)


def _fourier_branch(u, l, *, slabs_per_step=4):
    b, nt, ns, rows, _ = u.shape
    c = ns * LANES
    l1 = FOURIER_L1
    l2 = l // l1
    g = MXU_DIM // l2
    assert nt * SUBLANES == l2 and rows == l1 * SUBLANES and MXU_DIM % l2 == 0 and g % SUBLANES == 0
    assert ns % slabs_per_step == 0
    csw, cs1, g2 = _fourier_tables(l)
    blk = pl.BlockSpec((None, None, slabs_per_step, rows, LANES), lambda i, j, k: (i, j, k, 0, 0))
    z = pl.pallas_call(
        _fourier1_body,
        grid=(b, nt, ns // slabs_per_step),
        in_specs=[blk, _resident(cs1.shape)],
        out_specs=blk,
        out_shape=jax.ShapeDtypeStruct(u.shape, jnp.uint32),
        compiler_params=_params(("parallel", "parallel", "parallel")),
        name="fourier_stage1",
    )(u, cs1)
    y = pl.pallas_call(
        _fourier2_body,
        grid=(b, l1 // g),
        in_specs=[pl.BlockSpec((1, nt, ns, g * SUBLANES, LANES), lambda i, t: (i, 0, 0, t, 0)),
                  pl.BlockSpec((1, 2 * MXU_DIM, 2 * MXU_DIM), lambda i, t: (t, 0, 0)),
                  _resident(csw.shape)],
        out_specs=pl.BlockSpec((1, l2, g, c), lambda i, t: (i, 0, t, 0)),
        out_shape=jax.ShapeDtypeStruct((b, l2, l1, c), F32),
        compiler_params=_params(("parallel", "parallel")),
        name="fourier_stage2",
    )(z, g2, csw)
    return y.reshape(b, l, c)


CONV_HALO = BF16_SUBLANES


def _proj_main_body(xp_ref, x_ref, xn_ref, g_ref, w_ref, cw_ref, cb_ref, o_ref, h_ref, ext_ref, act_ref,
                    *, tm, tiles_per_seq):
    i, j = pl.program_id(0), pl.program_id(1)
    halo = CONV_HALO

    @pl.when(j == 0)
    def _():
        h_ref[0:halo, :] = _rms(xp_ref[...], g_ref[...]).astype(BF16)
        h_ref[halo:halo + tm, :] = _rms(x_ref[...], g_ref[...]).astype(BF16)
        h_ref[halo + tm:, :] = _rms(xn_ref[...], g_ref[...]).astype(BF16)

    seq_first = (i % tiles_per_seq) == 0
    seq_last = (i % tiles_per_seq) == tiles_per_seq - 1
    prod = jnp.dot(h_ref[...], w_ref[:, P_TILE:], preferred_element_type=F32)
    n_slabs = P_TILE // LANES
    for s in range(n_slabs):
        ext_ref[s] = prod[:, s * LANES:(s + 1) * LANES]
    ext_ref[:, 0:halo, :] = jnp.where(seq_first, 0.0, ext_ref[:, 0:halo, :])
    ext_ref[:, halo + tm:, :] = jnp.where(seq_last, 0.0, ext_ref[:, halo + tm:, :])

    o_ref[:, :P_TILE] = jnp.dot(h_ref[halo:halo + tm, :], w_ref[:, :P_TILE],
                                preferred_element_type=F32).astype(o_ref.dtype)
    n = tm // SUBLANES
    for s in range(n_slabs):
        lanes = slice(s * LANES, (s + 1) * LANES)
        ext, act = ext_ref.at[s], act_ref.at[s]
        taps = {m: ext[pl.ds(halo + m, n, stride=SUBLANES), :]
                for m in range(-CONV_PAD, SUBLANES + CONV_PAD)}
        for j in range(SUBLANES):
            acc = cb_ref[:, lanes]
            for k in range(CONV_WIDTH):
                acc = acc + cw_ref[k:k + 1, lanes] * taps[j + k - CONV_PAD]
            act[pl.ds(j, n, stride=SUBLANES), :] = acc * jax.nn.sigmoid(acc)
    for s in range(n_slabs):
        o_ref[:, P_TILE + s * LANES:P_TILE + (s + 1) * LANES] = act_ref[s].astype(o_ref.dtype)


def _proj_main(x, g, w, conv_w, conv_b, l, *, tm=1024):
    t, n = x.shape[0], w.shape[1]
    halo = CONV_HALO
    pair = 2 * P_TILE
    assert t % tm == 0 and l % tm == 0 and n == P_MAIN and tm % halo == 0
    assert P_PLAIN == tuple(range(0, 2 * len(P_PLAIN), 2)) and P_CONV == tuple(range(1, 2 * len(P_CONV), 2))
    hb = tm // halo
    nhb = t // halo
    return pl.pallas_call(
        functools.partial(_proj_main_body, tm=tm, tiles_per_seq=l // tm),
        grid=(t // tm, n // pair),
        in_specs=[pl.BlockSpec((halo, D_MODEL), lambda i, j: (jnp.maximum(i * hb - 1, 0), 0)),
                  pl.BlockSpec((tm, D_MODEL), lambda i, j: (i, 0)),
                  pl.BlockSpec((halo, D_MODEL), lambda i, j: (jnp.minimum((i + 1) * hb, nhb - 1), 0)),
                  pl.BlockSpec((1, D_MODEL), lambda i, j: (0, 0)),
                  pl.BlockSpec((D_MODEL, pair), lambda i, j: (0, j)),
                  pl.BlockSpec((CONV_WIDTH, P_TILE), lambda i, j: (0, j)),
                  pl.BlockSpec((1, P_TILE), lambda i, j: (0, j))],
        out_specs=pl.BlockSpec((tm, pair), lambda i, j: (i, j)),
        out_shape=jax.ShapeDtypeStruct((t, n), BF16),
        scratch_shapes=[pltpu.VMEM((tm + 2 * halo, D_MODEL), BF16),
                        pltpu.VMEM((P_TILE // LANES, tm + 2 * halo, LANES), F32),
                        pltpu.VMEM((P_TILE // LANES, tm, LANES), F32)],
        compiler_params=_params(("parallel", "arbitrary")),
        name="proj_main",
    )(x, x, x, g, w, conv_w, conv_b)


LOG2E = math.log2(math.e)

HEADS_PER_PAIR = LANES // SSM_HEAD_DIM
N_PAIRS = SSM_HEADS // HEADS_PER_PAIR
HEADS_PER_GROUP = SSM_HEADS // SSM_GROUPS
PAIRS_PER_GROUP = HEADS_PER_GROUP // HEADS_PER_PAIR


def _split3(a):
    hi = a.astype(BF16)
    r1 = a - hi.astype(F32)
    mid = r1.astype(BF16)
    lo = (r1 - mid.astype(F32)).astype(BF16)
    return hi, mid, lo


def _ssd_prep_body(dt_ref, bias_ref, alog_ref, cum_ref, rowb_ref, w_ref, *, chunks):
    q = CHUNK
    row = lax.broadcasted_iota(jnp.int32, (q, q), 0)
    col = lax.broadcasted_iota(jnp.int32, (q, q), 1)
    tri = jnp.concatenate([jnp.where(col <= row, 1.0, 0.0), jnp.where(col >= row, 1.0, 0.0)],
                          axis=0).astype(BF16)
    fwd_lane = lax.broadcasted_iota(jnp.int32, (q, LANES), 1) < SSM_HEADS
    neg_a = -jnp.exp(alog_ref[...]) * LOG2E
    for c in range(chunks):
        rows = slice(c * q, (c + 1) * q)
        dtv = jax.nn.softplus(dt_ref[0, rows, :] + bias_ref[...])
        a = dtv * neg_a
        both = sum(jnp.dot(tri, piece, preferred_element_type=F32) for piece in _split3(a))
        cum2 = jnp.where(fwd_lane, both[:q], both[q:])
        cend = jnp.where(fwd_lane[0:1], cum2[q - 1:q], cum2[0:1])
        cum_ref[0, rows, :] = cum2
        rowb_ref[0, rows, :] = (cum2 - jnp.log2(dtv)).T
        w_ref[0, rows, :] = (jnp.exp2(cend - cum2) * dtv).T


def _ssd_prep(dt_raw, bias, alog, *, chunks=4):
    b, l, _ = dt_raw.shape
    rows = chunks * CHUNK
    assert l % rows == 0
    spec = pl.BlockSpec((1, rows, DT_PAD), lambda n, c: (n, c, 0))
    vec = pl.BlockSpec((1, DT_PAD), lambda n, c: (0, 0))
    shape = jax.ShapeDtypeStruct((b, l, DT_PAD), F32)
    return pl.pallas_call(
        functools.partial(_ssd_prep_body, chunks=chunks),
        grid=(b, l // rows),
        in_specs=[spec, vec, vec],
        out_specs=[spec, spec, spec],
        out_shape=[shape, shape, shape],
        compiler_params=_params(("parallel", "parallel")),
        name="ssd_prep",
    )(dt_raw, bias, alog)


def _ssd_body(*refs, direction, final):
    if final:
        (x0_ref, x1_ref, b_ref, c_ref, cum_ref, rowb_ref, w_ref, yf_ref, z0_ref, z1_ref, dskip_ref,
         ng_ref, o_ref, s_ref, yacc_ref) = refs
    else:
        x0_ref, x1_ref, b_ref, c_ref, cum_ref, rowb_ref, w_ref, o_ref, s_ref = refs
    q = CHUNK
    pairs_per_tile = P_TILE // LANES

    @pl.when(pl.program_id(1) == 0)
    def _():
        s_ref[...] = jnp.zeros_like(s_ref)

    row = lax.broadcasted_iota(jnp.int32, (q, q), 0)
    col = lax.broadcasted_iota(jnp.int32, (q, q), 1)
    causal = (col <= row) if direction == 0 else (col >= row)
    end = q - 1 if direction == 0 else 0
    first_head = lax.broadcasted_iota(jnp.int32, (1, LANES), 1) < SSM_HEAD_DIM
    keep_bf = [jnp.where(first_head, 1.0, 0.0).astype(BF16),
               jnp.where(first_head, 0.0, 1.0).astype(BF16)]

    for pr in range(N_PAIRS):
        grp = pr // PAIRS_PER_GROUP
        if pr % PAIRS_PER_GROUP == 0:
            gcols = slice(grp * D_STATE, (grp + 1) * D_STATE)
            bg = b_ref[0, :, gcols]
            cg = c_ref[0, :, gcols]
            cb = lax.dot_general(cg, bg, (((1,), (1,)), ((), ())), preferred_element_type=F32)
            bg_t = bg.astype(F32).T
            s_grp = jnp.concatenate([s_ref[pr + k].astype(BF16) for k in range(PAIRS_PER_GROUP)], axis=1)
            y_off = jnp.dot(cg, s_grp, preferred_element_type=F32)
        x_ref = x0_ref if pr < pairs_per_tile else x1_ref
        xp = x_ref[0, :, (pr % pairs_per_tile) * LANES:(pr % pairs_per_tile + 1) * LANES]
        ms, bws, xs, ecols, cds = [], [], [], [], []
        for i in range(HEADS_PER_PAIR):
            hc = SSM_HEADS * direction + HEADS_PER_PAIR * pr + i
            colb = jnp.broadcast_to(cum_ref[0, :, hc:hc + 1], (q, LANES))
            m = jnp.where(causal, jnp.exp2(colb - rowb_ref[0, hc:hc + 1, :]), 0.0) * cb
            ms.append(m.astype(BF16))
            xs.append(xp * keep_bf[i])
            bws.append((bg_t * w_ref[0, hc:hc + 1, :]).astype(BF16))
            ecols.append(jnp.exp2(colb))
            cds.append(ecols[i][end:end + 1, :])
        x_split = jnp.concatenate(xs, axis=0)
        y_diag = jnp.dot(jnp.concatenate(ms, axis=1), x_split, preferred_element_type=F32)
        k = pr % PAIRS_PER_GROUP
        y_pair = y_diag + y_off[:, k * LANES:(k + 1) * LANES] * jnp.where(first_head, ecols[0], ecols[1])
        s_new = jnp.dot(jnp.concatenate(bws, axis=1), x_split, preferred_element_type=F32)
        s_ref[pr] = s_ref[pr] * jnp.where(first_head, cds[0], cds[1]) + s_new
        if final:
            yacc_ref[:, pr * LANES:(pr + 1) * LANES] = y_pair
        else:
            o_ref[0, :, pr * LANES:(pr + 1) * LANES] = y_pair.astype(o_ref.dtype)

    if final:
        x = jnp.concatenate([x0_ref[0], x1_ref[0]], axis=1).astype(F32)
        y = yacc_ref[...] + yf_ref[0].astype(F32) + x * dskip_ref[...]
        z = jnp.concatenate([z0_ref[0], z1_ref[0]], axis=1).astype(F32)
        o_ref[0] = _rms(y * (z * jax.nn.sigmoid(z)), ng_ref[...]).astype(o_ref.dtype)


def _ssd_pass(p_main, prep, direction, final_inputs=None):
    b, l, _ = p_main.shape
    nc = l // CHUNK
    assert l % CHUNK == 0 and D_INNER == 2 * P_TILE and SSM_GROUPS * D_STATE == P_TILE
    chunk = (lambda c: c) if direction == 0 else (lambda c: nc - 1 - c)
    tile = lambda k: pl.BlockSpec((1, CHUNK, P_TILE), lambda n, c: (n, chunk(c), k))
    small = pl.BlockSpec((1, CHUNK, DT_PAD), lambda n, c: (n, chunk(c), 0))
    in_specs = [tile(P_CONV[0]), tile(P_CONV[1]), tile(P_CONV[2]), tile(P_CONV[3]), small, small, small]
    args = [p_main, p_main, p_main, p_main, *prep]
    scratch = [pltpu.VMEM((N_PAIRS, D_STATE, LANES), F32)]
    final = final_inputs is not None
    if final:
        y_fwd, dskip, norm_g = final_inputs
        in_specs += [pl.BlockSpec((1, CHUNK, D_INNER), lambda n, c: (n, chunk(c), 0)),
                     tile(P_PLAIN[0]), tile(P_PLAIN[1]),
                     pl.BlockSpec((1, D_INNER), lambda n, c: (0, 0)),
                     pl.BlockSpec((1, D_INNER), lambda n, c: (0, 0))]
        args += [y_fwd, p_main, p_main, dskip, norm_g]
        scratch.append(pltpu.VMEM((CHUNK, D_INNER), F32))
    return pl.pallas_call(
        functools.partial(_ssd_body, direction=direction, final=final),
        grid=(b, nc),
        in_specs=in_specs,
        out_specs=pl.BlockSpec((1, CHUNK, D_INNER), lambda n, c: (n, chunk(c), 0)),
        out_shape=jax.ShapeDtypeStruct((b, l, D_INNER), BF16),
        scratch_shapes=scratch,
        compiler_params=_params(("parallel", "arbitrary")),
        name="ssd_bwd_norm" if final else "ssd_fwd",
    )(*args)


def _merge_body(x_ref, yf_ref, ym_ref, gf_ref, gm_ref, wf_ref, wm_ref, wo_ref, post_ref, o_ref):
    f = jnp.dot(yf_ref[...].astype(BF16), wf_ref[...], preferred_element_type=F32)
    m = jnp.dot(ym_ref[...], wm_ref[...], preferred_element_type=F32)
    merged = (jax.nn.sigmoid(gf_ref[...].astype(F32)) * f
              + jax.nn.sigmoid(gm_ref[...].astype(F32)) * m).astype(BF16)
    out = jnp.dot(merged, wo_ref[...], preferred_element_type=F32)
    o_ref[...] = x_ref[...] + _rms(out, post_ref[...])


def _merge_block(x, yf, ym, p_main, w_f, w_m, w_o, post_g, *, tm=512):
    t = x.shape[0]
    assert t % tm == 0
    assert P_TILE == D_MODEL
    return pl.pallas_call(
        _merge_body,
        grid=(t // tm,),
        in_specs=[pl.BlockSpec((tm, D_MODEL), lambda i: (i, 0)),
                  pl.BlockSpec((tm, D_FOURIER), lambda i: (i, 0)),
                  pl.BlockSpec((tm, D_INNER), lambda i: (i, 0)),
                  pl.BlockSpec((tm, D_MODEL), lambda i: (i, P_PLAIN[2])),
                  pl.BlockSpec((tm, D_MODEL), lambda i: (i, P_PLAIN[3])),
                  _resident((D_FOURIER, D_MODEL)), _resident((D_INNER, D_MODEL)),
                  _resident((D_MODEL, D_MODEL)), _resident((1, D_MODEL))],
        out_specs=pl.BlockSpec((tm, D_MODEL), lambda i: (i, 0)),
        out_shape=jax.ShapeDtypeStruct((t, D_MODEL), F32),
        compiler_params=_params(("parallel",)),
        name="merge_out",
    )(x, yf, ym, p_main, p_main, w_f, w_m, w_o, post_g)


def _prepare(ffn1_pre_g, ffn1_post_g, ffn1_w_in, ffn1_w_out, mix_pre_g, mix_post_g, w_in, conv_w,
             conv_b, dt_bias, a_log, d_skip, ssm_norm_g, w_branch_f, w_branch_m, w_out,
             ffn2_pre_g, ffn2_post_g, ffn2_w_in, ffn2_w_out):
    row = lambda v: v.reshape(1, -1).astype(F32)
    off_f, off_z = D_FOURIER, D_FOURIER + D_INNER
    off_xbc = off_z + CONV_DIM
    off_dt = off_xbc + 2 * SSM_HEADS
    plain = jnp.concatenate([w_in[:, off_f:off_z], w_in[:, off_dt:]], axis=1)
    conv = w_in[:, off_z:off_xbc]
    w_main = jnp.concatenate([part[:, k * P_TILE:(k + 1) * P_TILE]
                              for k in range(len(P_PLAIN)) for part in (plain, conv)], axis=1).astype(BF16)
    pad = DT_PAD - 2 * SSM_HEADS
    w_dt = jnp.pad(w_in[:, off_xbc:off_dt], ((0, 0), (0, pad))).astype(BF16)
    return dict(
        ffn1=(row(ffn1_pre_g), row(ffn1_post_g), ffn1_w_in.astype(BF16), ffn1_w_out.astype(BF16)),
        ffn2=(row(ffn2_pre_g), row(ffn2_post_g), ffn2_w_in.astype(BF16), ffn2_w_out.astype(BF16)),
        mix_pre=row(mix_pre_g), mix_post=row(mix_post_g),
        w_main=w_main, w_four=w_in[:, :off_f].astype(BF16), w_dt=w_dt,
        conv_w=conv_w.astype(F32), conv_b=row(conv_b),
        dt_bias=jnp.pad(row(dt_bias), ((0, 0), (0, pad))),
        a_log=jnp.pad(row(a_log), ((0, 0), (0, pad))),
        d_skip=row(jnp.repeat(d_skip, SSM_HEAD_DIM)), norm_g=row(ssm_norm_g),
        w_f=w_branch_f.astype(BF16), w_m=w_branch_m.astype(BF16), w_o=w_out.astype(BF16))


def _encoder_layer(x, p):
    b, l, d = x.shape
    t = b * l
    x = _ffn_block(x.reshape(t, d), *p["ffn1"])
    p_main = _proj_main(x, p["mix_pre"], p["w_main"], p["conv_w"], p["conv_b"], l).reshape(b, l, P_MAIN)
    u_f, dt_raw = _fourier_proj(x, p["mix_pre"], p["w_four"], p["w_dt"], b, l)
    dt_raw = dt_raw.reshape(b, l, DT_PAD)
    yf = _fourier_branch(u_f, l)
    prep = _ssd_prep(dt_raw, p["dt_bias"], p["a_log"])
    y_fwd = _ssd_pass(p_main, prep, 0)
    ym = _ssd_pass(p_main, prep, 1, final_inputs=(y_fwd, p["d_skip"], p["norm_g"]))
    x = _merge_block(x, yf.reshape(t, D_FOURIER), ym.reshape(t, D_INNER), p_main.reshape(t, P_MAIN),
                     p["w_f"], p["w_m"], p["w_o"], p["mix_post"])
    x = _ffn_block(x, *p["ffn2"])
    return x.reshape(b, l, d)


def kernel(x_prompt, x_sample, ffn1_pre_g, ffn1_post_g, ffn1_w_in, ffn1_w_out, mix_pre_g, mix_post_g, w_in, conv_w, conv_b, dt_bias, a_log, d_skip, ssm_norm_g, w_branch_f, w_branch_m, w_out, ffn2_pre_g, ffn2_post_g, ffn2_w_in, ffn2_w_out):
    layers = (ffn1_pre_g, ffn1_post_g, ffn1_w_in, ffn1_w_out, mix_pre_g, mix_post_g, w_in, conv_w,
              conv_b, dt_bias, a_log, d_skip, ssm_norm_g, w_branch_f, w_branch_m, w_out,
              ffn2_pre_g, ffn2_post_g, ffn2_w_in, ffn2_w_out)
    y_prompt, y_sample = x_prompt, x_sample
    for layer in range(ffn1_pre_g.shape[0]):
        p = _prepare(*(a[layer] for a in layers))
        y_prompt = _encoder_layer(y_prompt, p)
        y_sample = _encoder_layer(y_sample, p)
    return (y_prompt, y_sample)
```

```python
import functools
import math

import numpy as np
import jax
import jax.numpy as jnp
from jax import lax
from jax.experimental import pallas as pl
from jax.experimental.pallas import tpu as pltpu

F32 = jnp.float32
BF16 = jnp.bfloat16

D_MODEL = 1024
D_FF = 2816
FOURIER_GROUP_WIDTH = 256
D_FOURIER = 1024
SSM_HEADS = 32
SSM_HEAD_DIM = 64
D_INNER = SSM_HEADS * SSM_HEAD_DIM
SSM_GROUPS = 8
D_STATE = 128
CONV_WIDTH = 5
CONV_PAD = CONV_WIDTH // 2
CHUNK = 128
CONV_DIM = D_INNER + 2 * SSM_GROUPS * D_STATE
EPS = 1e-6

LANES = 128
MXU_DIM = 256
BF16_SUBLANES = 16
VMEM_LIMIT = 56 * 1024 * 1024

P_TILE = 1024
P_PLAIN = (0, 2, 4, 6)
P_CONV = (1, 3, 5, 7)
P_MAIN = (len(P_PLAIN) + len(P_CONV)) * P_TILE
DT_PAD = LANES


def _params(sem):
    return pltpu.CompilerParams(dimension_semantics=sem, vmem_limit_bytes=VMEM_LIMIT)


def _rms(x, g):
    return x * lax.rsqrt(jnp.mean(x * x, axis=-1, keepdims=True) + EPS) * g


def _resident(shape):
    nd = len(shape)
    return pl.BlockSpec(shape, lambda *_: (0,) * nd, pipeline_mode=pl.Buffered(1))


def _ffn_body(x_ref, pre_ref, post_ref, win_ref, wout_ref, o_ref, h_ref, *, tf):
    x = x_ref[...]
    h_ref[...] = _rms(x, pre_ref[...]).astype(BF16)
    acc = None
    for c in range(D_FF // tf):
        h = h_ref[...]
        gate = jnp.dot(h, win_ref[:, c * tf:(c + 1) * tf], preferred_element_type=F32)
        up = jnp.dot(h, win_ref[:, D_FF + c * tf:D_FF + (c + 1) * tf], preferred_element_type=F32)
        act = (gate * jax.nn.sigmoid(gate) * up).astype(BF16)
        part = jnp.dot(act, wout_ref[c * tf:(c + 1) * tf, :], preferred_element_type=F32)
        acc = part if acc is None else acc + part
    o_ref[...] = x + 0.5 * _rms(acc, post_ref[...])


def _ffn_block(x, pre_g, post_g, w_in, w_out, *, tm=512, tf=256):
    t = x.shape[0]
    assert t % tm == 0 and D_FF % tf == 0
    return pl.pallas_call(
        functools.partial(_ffn_body, tf=tf),
        grid=(t // tm,),
        in_specs=[pl.BlockSpec((tm, D_MODEL), lambda i: (i, 0)),
                  _resident((1, D_MODEL)), _resident((1, D_MODEL)),
                  _resident((D_MODEL, 2 * D_FF)), _resident((D_FF, D_MODEL))],
        out_specs=pl.BlockSpec((tm, D_MODEL), lambda i: (i, 0)),
        out_shape=jax.ShapeDtypeStruct((t, D_MODEL), F32),
        scratch_shapes=[pltpu.VMEM((tm, D_MODEL), BF16)],
        compiler_params=_params(("parallel",)),
        name="ffn_block",
    )(x, pre_g, post_g, w_in, w_out)


FOURIER_L1 = MXU_DIM
SUBLANES = 8


def _fourier_tables(l):
    l1 = FOURIER_L1
    l2 = l // l1
    g = MXU_DIM // l2
    w = FOURIER_GROUP_WIDTH
    c = np.arange(w)
    ang = 2.0 * np.pi * np.outer(c, c) / w
    cs_w = np.concatenate([np.cos(ang), -np.sin(ang)], axis=0) / math.sqrt(w)
    n1 = np.arange(l1)
    ang1 = 2.0 * np.pi * np.outer(n1, n1) / l1
    cs1 = np.concatenate([np.cos(ang1), np.sin(ang1)], axis=0)
    k1 = np.arange(l1).reshape(l1 // g, 1, g, 1, 1)
    k2 = np.arange(l2).reshape(1, l2, 1, 1, 1)
    n2 = np.arange(l2).reshape(1, 1, 1, 1, l2)
    theta = 2.0 * np.pi * (n2 * k2 / l2 + n2 * k1 / l)
    eye = np.eye(g).reshape(1, 1, g, g, 1)

    def tiled_cols(m):
        m = m.reshape(l1 // g, l2 * g, g, l2 // SUBLANES, SUBLANES)
        return m.transpose(0, 1, 3, 2, 4).reshape(l1 // g, l2 * g, g * l2)

    gc = tiled_cols(np.cos(theta) * eye)
    gs = tiled_cols(np.sin(theta) * eye)
    g2 = np.concatenate([np.concatenate([gc, -gs], axis=2),
                         np.concatenate([gs, gc], axis=2)], axis=1) / math.sqrt(l)
    as_bf16 = lambda a: jnp.asarray(a, dtype=F32).astype(BF16)
    return as_bf16(cs_w), as_bf16(cs1), as_bf16(g2)


def _fourier_proj_body(x_ref, g_ref, w_ref, wdt_ref, o_ref, dt_ref, *, l2):
    h = _rms(x_ref[...], g_ref[...]).astype(BF16)
    dt_ref[...] = jnp.dot(h, wdt_ref[...], preferred_element_type=F32)
    u = jnp.dot(h, w_ref[...], preferred_element_type=F32)
    tm, tn = u.shape
    u = u.reshape(tm // l2, l2 // SUBLANES, SUBLANES, tn)
    for t in range(l2 // SUBLANES):
        ut = u[:, t].reshape(tm // l2 * SUBLANES, tn)
        for s in range(tn // LANES):
            o_ref[0, t, s] = ut[:, s * LANES:(s + 1) * LANES]


def _fourier_proj(x, g, w, w_dt, b, l, *, tm=1024):
    t, c = x.shape[0], w.shape[1]
    l1 = FOURIER_L1
    l2 = l // l1
    per_seq = l // tm
    assert t == b * l and l % tm == 0 and tm % l2 == 0 and l2 % SUBLANES == 0 and c % LANES == 0
    return pl.pallas_call(
        functools.partial(_fourier_proj_body, l2=l2),
        grid=(t // tm,),
        in_specs=[pl.BlockSpec((tm, D_MODEL), lambda i: (i, 0)),
                  _resident((1, D_MODEL)), _resident((D_MODEL, c)), _resident((D_MODEL, DT_PAD))],
        out_specs=[pl.BlockSpec((1, l2 // SUBLANES, c // LANES, tm // l2 * SUBLANES, LANES),
                                lambda i: (i // per_seq, 0, 0, i % per_seq, 0)),
                   pl.BlockSpec((tm, DT_PAD), lambda i: (i, 0))],
        out_shape=[jax.ShapeDtypeStruct((b, l2 // SUBLANES, c // LANES, l1 * SUBLANES, LANES), F32),
                   jax.ShapeDtypeStruct((t, DT_PAD), F32)],
        compiler_params=_params(("parallel",)),
        name="fourier_proj",
    )(x, g, w, w_dt)


def _pack_bf16_pair(hi, lo):
    bits = lambda v: lax.bitcast_convert_type(v.astype(BF16).astype(F32), jnp.uint32)
    return bits(hi) | (bits(lo) >> 16)


def _unpack_bf16_pair(word):
    hi = lax.bitcast_convert_type(word & jnp.uint32(0xFFFF0000), F32).astype(BF16)
    lo = lax.bitcast_convert_type(word << 16, F32).astype(BF16)
    return hi, lo


def _fourier1_body(u_ref, cs1_ref, z_ref):
    l1 = FOURIER_L1
    slabs = u_ref.shape[0]
    for j in range(SUBLANES):
        pick = pl.ds(j, l1, stride=SUBLANES)
        u = jnp.concatenate([u_ref.at[s][pick, :] for s in range(slabs)], axis=1).astype(BF16)
        z = jnp.dot(cs1_ref[...], u, preferred_element_type=F32)
        word = _pack_bf16_pair(z[:l1], z[l1:])
        for s in range(slabs):
            z_ref.at[s][pick, :] = word[:, s * LANES:(s + 1) * LANES]


def _fourier2_body(z_ref, g2_ref, csw_ref, o_ref, *, tiles, g):
    w = FOURIER_GROUP_WIDTH
    half = MXU_DIM
    slabs = z_ref.shape[2]
    rows = g * SUBLANES
    for t in range(tiles):
        parts = [_unpack_bf16_pair(z_ref[0, :, s, t * rows:(t + 1) * rows, :].reshape(half, LANES))
                 for s in range(slabs)]
        z = jnp.concatenate([jnp.concatenate([p[0] for p in parts], axis=1),
                             jnp.concatenate([p[1] for p in parts], axis=1)], axis=0)
        v = jnp.dot(g2_ref[t], z, preferred_element_type=F32).astype(BF16)
        for s in range(z.shape[1] // w):
            cols = slice(s * w, (s + 1) * w)
            vv = jnp.concatenate([v[:half, cols], v[half:, cols]], axis=1)
            res = jnp.dot(vv, csw_ref[...], preferred_element_type=F32)
            o_ref[0, :, t * g:(t + 1) * g, cols] = res.reshape(o_ref.shape[1], g, w)


def _fourier_branch(u, l, *, slabs_per_step=4, tiles_per_step=2):
    b, nt, ns, rows, _ = u.shape
    c = ns * LANES
    l1 = FOURIER_L1
    l2 = l // l1
    g = MXU_DIM // l2
    assert nt * SUBLANES == l2 and rows == l1 * SUBLANES and MXU_DIM % l2 == 0 and g % SUBLANES == 0
    assert ns % slabs_per_step == 0
    csw, cs1, g2 = _fourier_tables(l)
    blk = pl.BlockSpec((None, None, slabs_per_step, rows, LANES), lambda i, j, k: (i, j, k, 0, 0))
    z = pl.pallas_call(
        _fourier1_body,
        grid=(b, nt, ns // slabs_per_step),
        in_specs=[blk, _resident(cs1.shape)],
        out_specs=blk,
        out_shape=jax.ShapeDtypeStruct(u.shape, jnp.uint32),
        compiler_params=_params(("parallel", "parallel", "parallel")),
        name="fourier_stage1",
    )(u, cs1)
    tiles = tiles_per_step
    assert (l1 // g) % tiles == 0
    y = pl.pallas_call(
        functools.partial(_fourier2_body, tiles=tiles, g=g),
        grid=(b, l1 // g // tiles),
        in_specs=[pl.BlockSpec((1, nt, ns, tiles * g * SUBLANES, LANES), lambda i, t: (i, 0, 0, t, 0)),
                  pl.BlockSpec((tiles, 2 * MXU_DIM, 2 * MXU_DIM), lambda i, t: (t, 0, 0)),
                  _resident(csw.shape)],
        out_specs=pl.BlockSpec((1, l2, tiles * g, c), lambda i, t: (i, 0, t, 0)),
        out_shape=jax.ShapeDtypeStruct((b, l2, l1, c), F32),
        compiler_params=_params(("parallel", "parallel")),
        name="fourier_stage2",
    )(z, g2, csw)
    return y.reshape(b, l, c)


CONV_HALO = BF16_SUBLANES
CONV_PHASES = 4


def _proj_main_body(xp_ref, x_ref, xn_ref, g_ref, w_ref, cw_ref, cb_ref, o_ref, h_ref, ext_ref, act_ref,
                    *, tm, tiles_per_seq):
    i, j = pl.program_id(0), pl.program_id(1)
    halo = CONV_HALO

    @pl.when(j == 0)
    def _():
        h_ref[0:halo, :] = _rms(xp_ref[...], g_ref[...]).astype(BF16)
        h_ref[halo:halo + tm, :] = _rms(x_ref[...], g_ref[...]).astype(BF16)
        h_ref[halo + tm:, :] = _rms(xn_ref[...], g_ref[...]).astype(BF16)

    seq_first = (i % tiles_per_seq) == 0
    seq_last = (i % tiles_per_seq) == tiles_per_seq - 1
    prod = jnp.dot(h_ref[...], w_ref[:, P_TILE:], preferred_element_type=F32)
    n_slabs = P_TILE // LANES
    for s in range(n_slabs):
        ext_ref[s] = prod[:, s * LANES:(s + 1) * LANES]
    ext_ref[:, 0:halo, :] = jnp.where(seq_first, 0.0, ext_ref[:, 0:halo, :])
    ext_ref[:, halo + tm:, :] = jnp.where(seq_last, 0.0, ext_ref[:, halo + tm:, :])

    o_ref[:, :P_TILE] = jnp.dot(h_ref[halo:halo + tm, :], w_ref[:, :P_TILE],
                                preferred_element_type=F32).astype(o_ref.dtype)
    n = tm // CONV_PHASES
    for s in range(n_slabs):
        lanes = slice(s * LANES, (s + 1) * LANES)
        ext, act = ext_ref.at[s], act_ref.at[s]
        taps = {m: ext[pl.ds(halo + m, n, stride=CONV_PHASES), :]
                for m in range(-CONV_PAD, CONV_PHASES + CONV_PAD)}
        for j in range(CONV_PHASES):
            acc = cb_ref[:, lanes]
            for k in range(CONV_WIDTH):
                acc = acc + cw_ref[k:k + 1, lanes] * taps[j + k - CONV_PAD]
            act[pl.ds(j, n, stride=CONV_PHASES), :] = acc * jax.nn.sigmoid(acc)
    for s in range(n_slabs):
        o_ref[:, P_TILE + s * LANES:P_TILE + (s + 1) * LANES] = act_ref[s].astype(o_ref.dtype)


def _proj_main(x, g, w, conv_w, conv_b, l, *, tm=1024):
    t, n = x.shape[0], w.shape[1]
    halo = CONV_HALO
    pair = 2 * P_TILE
    assert t % tm == 0 and l % tm == 0 and n == P_MAIN and tm % halo == 0
    assert P_PLAIN == tuple(range(0, 2 * len(P_PLAIN), 2)) and P_CONV == tuple(range(1, 2 * len(P_CONV), 2))
    hb = tm // halo
    nhb = t // halo
    return pl.pallas_call(
        functools.partial(_proj_main_body, tm=tm, tiles_per_seq=l // tm),
        grid=(t // tm, n // pair),
        in_specs=[pl.BlockSpec((halo, D_MODEL), lambda i, j: (jnp.maximum(i * hb - 1, 0), 0)),
                  pl.BlockSpec((tm, D_MODEL), lambda i, j: (i, 0)),
                  pl.BlockSpec((halo, D_MODEL), lambda i, j: (jnp.minimum((i + 1) * hb, nhb - 1), 0)),
                  pl.BlockSpec((1, D_MODEL), lambda i, j: (0, 0)),
                  pl.BlockSpec((D_MODEL, pair), lambda i, j: (0, j)),
                  pl.BlockSpec((CONV_WIDTH, P_TILE), lambda i, j: (0, j)),
                  pl.BlockSpec((1, P_TILE), lambda i, j: (0, j))],
        out_specs=pl.BlockSpec((tm, pair), lambda i, j: (i, j)),
        out_shape=jax.ShapeDtypeStruct((t, n), BF16),
        scratch_shapes=[pltpu.VMEM((tm + 2 * halo, D_MODEL), BF16),
                        pltpu.VMEM((P_TILE // LANES, tm + 2 * halo, LANES), F32),
                        pltpu.VMEM((P_TILE // LANES, tm, LANES), F32)],
        compiler_params=_params(("parallel", "arbitrary")),
        name="proj_main",
    )(x, x, x, g, w, conv_w, conv_b)


LOG2E = math.log2(math.e)

HEADS_PER_PAIR = LANES // SSM_HEAD_DIM
N_PAIRS = SSM_HEADS // HEADS_PER_PAIR
HEADS_PER_GROUP = SSM_HEADS // SSM_GROUPS
PAIRS_PER_GROUP = HEADS_PER_GROUP // HEADS_PER_PAIR


def _split3(a):
    hi = a.astype(BF16)
    r1 = a - hi.astype(F32)
    mid = r1.astype(BF16)
    lo = (r1 - mid.astype(F32)).astype(BF16)
    return hi, mid, lo


def _ssd_prep_body(dt_ref, bias_ref, alog_ref, cum_ref, rowb_ref, w_ref, *, chunks):
    q = CHUNK
    row = lax.broadcasted_iota(jnp.int32, (q, q), 0)
    col = lax.broadcasted_iota(jnp.int32, (q, q), 1)
    tri = jnp.concatenate([jnp.where(col <= row, 1.0, 0.0), jnp.where(col >= row, 1.0, 0.0)],
                          axis=0).astype(BF16)
    fwd_lane = lax.broadcasted_iota(jnp.int32, (q, LANES), 1) < SSM_HEADS
    neg_a = -jnp.exp(alog_ref[...]) * LOG2E
    for c in range(chunks):
        rows = slice(c * q, (c + 1) * q)
        dtv = jax.nn.softplus(dt_ref[0, rows, :] + bias_ref[...])
        a = dtv * neg_a
        both = sum(jnp.dot(tri, piece, preferred_element_type=F32) for piece in _split3(a))
        cum2 = jnp.where(fwd_lane, both[:q], both[q:])
        cend = jnp.where(fwd_lane[0:1], cum2[q - 1:q], cum2[0:1])
        cum_ref[0, rows, :] = cum2
        rowb_ref[0, rows, :] = (cum2 - jnp.log2(dtv)).T
        w_ref[0, rows, :] = (jnp.exp2(cend - cum2) * dtv).T


def _ssd_prep(dt_raw, bias, alog, *, chunks=4):
    b, l, _ = dt_raw.shape
    rows = chunks * CHUNK
    assert l % rows == 0
    spec = pl.BlockSpec((1, rows, DT_PAD), lambda n, c: (n, c, 0))
    vec = pl.BlockSpec((1, DT_PAD), lambda n, c: (0, 0))
    shape = jax.ShapeDtypeStruct((b, l, DT_PAD), F32)
    return pl.pallas_call(
        functools.partial(_ssd_prep_body, chunks=chunks),
        grid=(b, l // rows),
        in_specs=[spec, vec, vec],
        out_specs=[spec, spec, spec],
        out_shape=[shape, shape, shape],
        compiler_params=_params(("parallel", "parallel")),
        name="ssd_prep",
    )(dt_raw, bias, alog)


def _ssd_body(*refs, direction, final, cps):
    if final:
        (x0_ref, x1_ref, b_ref, c_ref, cum_ref, rowb_ref, w_ref, yf_ref, z0_ref, z1_ref, dskip_ref,
         ng_ref, o_ref, s_ref, yacc_ref) = refs
    else:
        x0_ref, x1_ref, b_ref, c_ref, cum_ref, rowb_ref, w_ref, o_ref, s_ref = refs
    q = CHUNK
    pairs_per_tile = P_TILE // LANES

    @pl.when(pl.program_id(1) == 0)
    def _():
        s_ref[...] = jnp.zeros_like(s_ref)

    row = lax.broadcasted_iota(jnp.int32, (q, q), 0)
    col = lax.broadcasted_iota(jnp.int32, (q, q), 1)
    causal = (col <= row) if direction == 0 else (col >= row)
    end = q - 1 if direction == 0 else 0
    first_head = lax.broadcasted_iota(jnp.int32, (1, LANES), 1) < SSM_HEAD_DIM
    keep_bf = [jnp.where(first_head, 1.0, 0.0).astype(BF16),
               jnp.where(first_head, 0.0, 1.0).astype(BF16)]

    order = range(cps) if direction == 0 else range(cps - 1, -1, -1)
    for grp in range(SSM_GROUPS):
        gcols = slice(grp * D_STATE, (grp + 1) * D_STATE)
        for ck in order:
            r0 = ck * q
            rows = slice(r0, r0 + q)
            bg = b_ref[0, rows, gcols]
            cg = c_ref[0, rows, gcols]
            cb = lax.dot_general(cg, bg, (((1,), (1,)), ((), ())), preferred_element_type=F32)
            bg_t = bg.astype(F32).T
            first_pair = grp * PAIRS_PER_GROUP
            s_grp = jnp.concatenate([s_ref[first_pair + k].astype(BF16) for k in range(PAIRS_PER_GROUP)],
                                    axis=1)
            y_off = jnp.dot(cg, s_grp, preferred_element_type=F32)
            for k in range(PAIRS_PER_GROUP):
                pr = first_pair + k
                x_ref = x0_ref if pr < pairs_per_tile else x1_ref
                xp = x_ref[0, rows, (pr % pairs_per_tile) * LANES:(pr % pairs_per_tile + 1) * LANES]
                ms, bws, xs, ecols, cds = [], [], [], [], []
                for i in range(HEADS_PER_PAIR):
                    hc = SSM_HEADS * direction + HEADS_PER_PAIR * pr + i
                    colb = jnp.broadcast_to(cum_ref[0, rows, hc:hc + 1], (q, LANES))
                    m = jnp.where(causal, jnp.exp2(colb - rowb_ref[0, r0 + hc:r0 + hc + 1, :]), 0.0) * cb
                    ms.append(m.astype(BF16))
                    xs.append(xp * keep_bf[i])
                    bws.append((bg_t * w_ref[0, r0 + hc:r0 + hc + 1, :]).astype(BF16))
                    ecols.append(jnp.exp2(colb))
                    cds.append(ecols[i][end:end + 1, :])
                x_split = jnp.concatenate(xs, axis=0)
                y_diag = jnp.dot(jnp.concatenate(ms, axis=1), x_split, preferred_element_type=F32)
                y_pair = (y_diag + y_off[:, k * LANES:(k + 1) * LANES]
                          * jnp.where(first_head, ecols[0], ecols[1]))
                s_new = jnp.dot(jnp.concatenate(bws, axis=1), x_split, preferred_element_type=F32)
                s_ref[pr] = s_ref[pr] * jnp.where(first_head, cds[0], cds[1]) + s_new
                if final:
                    yacc_ref[rows, pr * LANES:(pr + 1) * LANES] = y_pair
                else:
                    o_ref[0, rows, pr * LANES:(pr + 1) * LANES] = y_pair.astype(o_ref.dtype)

    if final:
        x = jnp.concatenate([x0_ref[0], x1_ref[0]], axis=1).astype(F32)
        y = yacc_ref[...] + yf_ref[0].astype(F32) + x * dskip_ref[...]
        z = jnp.concatenate([z0_ref[0], z1_ref[0]], axis=1).astype(F32)
        o_ref[0] = _rms(y * (z * jax.nn.sigmoid(z)), ng_ref[...]).astype(o_ref.dtype)


def _ssd_pass(p_main, prep, direction, final_inputs=None, *, cps=4):
    b, l, _ = p_main.shape
    rows = cps * CHUNK
    nc = l // rows
    assert l % rows == 0 and D_INNER == 2 * P_TILE and SSM_GROUPS * D_STATE == P_TILE
    chunk = (lambda c: c) if direction == 0 else (lambda c: nc - 1 - c)
    tile = lambda k: pl.BlockSpec((1, rows, P_TILE), lambda n, c: (n, chunk(c), k))
    small = pl.BlockSpec((1, rows, DT_PAD), lambda n, c: (n, chunk(c), 0))
    in_specs = [tile(P_CONV[0]), tile(P_CONV[1]), tile(P_CONV[2]), tile(P_CONV[3]), small, small, small]
    args = [p_main, p_main, p_main, p_main, *prep]
    scratch = [pltpu.VMEM((N_PAIRS, D_STATE, LANES), F32)]
    final = final_inputs is not None
    if final:
        y_fwd, dskip, norm_g = final_inputs
        in_specs += [pl.BlockSpec((1, rows, D_INNER), lambda n, c: (n, chunk(c), 0)),
                     tile(P_PLAIN[0]), tile(P_PLAIN[1]),
                     pl.BlockSpec((1, D_INNER), lambda n, c: (0, 0)),
                     pl.BlockSpec((1, D_INNER), lambda n, c: (0, 0))]
        args += [y_fwd, p_main, p_main, dskip, norm_g]
        scratch.append(pltpu.VMEM((rows, D_INNER), F32))
    return pl.pallas_call(
        functools.partial(_ssd_body, direction=direction, final=final, cps=cps),
        grid=(b, nc),
        in_specs=in_specs,
        out_specs=pl.BlockSpec((1, rows, D_INNER), lambda n, c: (n, chunk(c), 0)),
        out_shape=jax.ShapeDtypeStruct((b, l, D_INNER), BF16),
        scratch_shapes=scratch,
        compiler_params=_params(("parallel", "arbitrary")),
        name="ssd_bwd_norm" if final else "ssd_fwd",
    )(*args)


def _merge_body(x_ref, yf_ref, ym_ref, gf_ref, gm_ref, wf_ref, wm_ref, wo_ref, post_ref, o_ref):
    f = jnp.dot(yf_ref[...].astype(BF16), wf_ref[...], preferred_element_type=F32)
    m = jnp.dot(ym_ref[...], wm_ref[...], preferred_element_type=F32)
    merged = (jax.nn.sigmoid(gf_ref[...].astype(F32)) * f
              + jax.nn.sigmoid(gm_ref[...].astype(F32)) * m).astype(BF16)
    out = jnp.dot(merged, wo_ref[...], preferred_element_type=F32)
    o_ref[...] = x_ref[...] + _rms(out, post_ref[...])


def _merge_block(x, yf, ym, p_main, w_f, w_m, w_o, post_g, *, tm=512):
    t = x.shape[0]
    assert t % tm == 0
    assert P_TILE == D_MODEL
    return pl.pallas_call(
        _merge_body,
        grid=(t // tm,),
        in_specs=[pl.BlockSpec((tm, D_MODEL), lambda i: (i, 0)),
                  pl.BlockSpec((tm, D_FOURIER), lambda i: (i, 0)),
                  pl.BlockSpec((tm, D_INNER), lambda i: (i, 0)),
                  pl.BlockSpec((tm, D_MODEL), lambda i: (i, P_PLAIN[2])),
                  pl.BlockSpec((tm, D_MODEL), lambda i: (i, P_PLAIN[3])),
                  _resident((D_FOURIER, D_MODEL)), _resident((D_INNER, D_MODEL)),
                  _resident((D_MODEL, D_MODEL)), _resident((1, D_MODEL))],
        out_specs=pl.BlockSpec((tm, D_MODEL), lambda i: (i, 0)),
        out_shape=jax.ShapeDtypeStruct((t, D_MODEL), F32),
        compiler_params=_params(("parallel",)),
        name="merge_out",
    )(x, yf, ym, p_main, p_main, w_f, w_m, w_o, post_g)


def _prepare(ffn1_pre_g, ffn1_post_g, ffn1_w_in, ffn1_w_out, mix_pre_g, mix_post_g, w_in, conv_w,
             conv_b, dt_bias, a_log, d_skip, ssm_norm_g, w_branch_f, w_branch_m, w_out,
             ffn2_pre_g, ffn2_post_g, ffn2_w_in, ffn2_w_out):
    row = lambda v: v.reshape(1, -1).astype(F32)
    off_f, off_z = D_FOURIER, D_FOURIER + D_INNER
    off_xbc = off_z + CONV_DIM
    off_dt = off_xbc + 2 * SSM_HEADS
    plain = jnp.concatenate([w_in[:, off_f:off_z], w_in[:, off_dt:]], axis=1)
    conv = w_in[:, off_z:off_xbc]
    w_main = jnp.concatenate([part[:, k * P_TILE:(k + 1) * P_TILE]
                              for k in range(len(P_PLAIN)) for part in (plain, conv)], axis=1).astype(BF16)
    pad = DT_PAD - 2 * SSM_HEADS
    w_dt = jnp.pad(w_in[:, off_xbc:off_dt], ((0, 0), (0, pad))).astype(BF16)
    return dict(
        ffn1=(row(ffn1_pre_g), row(ffn1_post_g), ffn1_w_in.astype(BF16), ffn1_w_out.astype(BF16)),
        ffn2=(row(ffn2_pre_g), row(ffn2_post_g), ffn2_w_in.astype(BF16), ffn2_w_out.astype(BF16)),
        mix_pre=row(mix_pre_g), mix_post=row(mix_post_g),
        w_main=w_main, w_four=w_in[:, :off_f].astype(BF16), w_dt=w_dt,
        conv_w=conv_w.astype(F32), conv_b=row(conv_b),
        dt_bias=jnp.pad(row(dt_bias), ((0, 0), (0, pad))),
        a_log=jnp.pad(row(a_log), ((0, 0), (0, pad))),
        d_skip=row(jnp.repeat(d_skip, SSM_HEAD_DIM)), norm_g=row(ssm_norm_g),
        w_f=w_branch_f.astype(BF16), w_m=w_branch_m.astype(BF16), w_o=w_out.astype(BF16))


def _encoder_layer(x, p):
    b, l, d = x.shape
    t = b * l
    x = _ffn_block(x.reshape(t, d), *p["ffn1"])
    p_main = _proj_main(x, p["mix_pre"], p["w_main"], p["conv_w"], p["conv_b"], l).reshape(b, l, P_MAIN)
    u_f, dt_raw = _fourier_proj(x, p["mix_pre"], p["w_four"], p["w_dt"], b, l)
    dt_raw = dt_raw.reshape(b, l, DT_PAD)
    yf = _fourier_branch(u_f, l)
    prep = _ssd_prep(dt_raw, p["dt_bias"], p["a_log"])
    y_fwd = _ssd_pass(p_main, prep, 0)
    ym = _ssd_pass(p_main, prep, 1, final_inputs=(y_fwd, p["d_skip"], p["norm_g"]))
    x = _merge_block(x, yf.reshape(t, D_FOURIER), ym.reshape(t, D_INNER), p_main.reshape(t, P_MAIN),
                     p["w_f"], p["w_m"], p["w_o"], p["mix_post"])
    x = _ffn_block(x, *p["ffn2"])
    return x.reshape(b, l, d)


def kernel(x_prompt, x_sample, ffn1_pre_g, ffn1_post_g, ffn1_w_in, ffn1_w_out, mix_pre_g, mix_post_g, w_in, conv_w, conv_b, dt_bias, a_log, d_skip, ssm_norm_g, w_branch_f, w_branch_m, w_out, ffn2_pre_g, ffn2_post_g, ffn2_w_in, ffn2_w_out):
    layers = (ffn1_pre_g, ffn1_post_g, ffn1_w_in, ffn1_w_out, mix_pre_g, mix_post_g, w_in, conv_w,
              conv_b, dt_bias, a_log, d_skip, ssm_norm_g, w_branch_f, w_branch_m, w_out,
              ffn2_pre_g, ffn2_post_g, ffn2_w_in, ffn2_w_out)
    y_prompt, y_sample = x_prompt, x_sample
    for layer in range(ffn1_pre_g.shape[0]):
        p = _prepare(*(a[layer] for a in layers))
        y_prompt = _encoder_layer(y_prompt, p)
        y_sample = _encoder_layer(y_sample, p)
    return (y_prompt, y_sample)
```

```python
import functools
import math

import numpy as np
import jax
import jax.numpy as jnp
from jax import lax
from jax.experimental import pallas as pl
from jax.experimental.pallas import tpu as pltpu

F32 = jnp.float32
BF16 = jnp.bfloat16

D_MODEL = 1024
D_FF = 2816
FOURIER_GROUP_WIDTH = 256
D_FOURIER = 1024
SSM_HEADS = 32
SSM_HEAD_DIM = 64
D_INNER = SSM_HEADS * SSM_HEAD_DIM
SSM_GROUPS = 8
D_STATE = 128
CONV_WIDTH = 5
CONV_PAD = CONV_WIDTH // 2
CHUNK = 128
CONV_DIM = D_INNER + 2 * SSM_GROUPS * D_STATE
EPS = 1e-6

LANES = 128
MXU_DIM = 256
BF16_SUBLANES = 16
VMEM_LIMIT = 56 * 1024 * 1024

P_TILE = 1024
P_PLAIN = (0, 2, 4, 6)
P_CONV = (1, 3, 5, 7)
P_MAIN = (len(P_PLAIN) + len(P_CONV)) * P_TILE
DT_PAD = LANES


def _params(sem):
    return pltpu.CompilerParams(dimension_semantics=sem, vmem_limit_bytes=VMEM_LIMIT)


def _rms(x, g):
    return x * lax.rsqrt(jnp.mean(x * x, axis=-1, keepdims=True) + EPS) * g


def _resident(shape):
    nd = len(shape)
    return pl.BlockSpec(shape, lambda *_: (0,) * nd, pipeline_mode=pl.Buffered(1))


def _ffn_body(x_ref, pre_ref, post_ref, win_ref, wout_ref, o_ref, h_ref, *, tf):
    x = x_ref[...]
    h_ref[...] = _rms(x, pre_ref[...]).astype(BF16)
    acc = None
    for c in range(D_FF // tf):
        h = h_ref[...]
        gate = jnp.dot(h, win_ref[:, c * tf:(c + 1) * tf], preferred_element_type=F32)
        up = jnp.dot(h, win_ref[:, D_FF + c * tf:D_FF + (c + 1) * tf], preferred_element_type=F32)
        act = (gate * jax.nn.sigmoid(gate) * up).astype(BF16)
        part = jnp.dot(act, wout_ref[c * tf:(c + 1) * tf, :], preferred_element_type=F32)
        acc = part if acc is None else acc + part
    o_ref[...] = x + 0.5 * _rms(acc, post_ref[...])


def _ffn_block(x, pre_g, post_g, w_in, w_out, *, tm=512, tf=256):
    t = x.shape[0]
    assert t % tm == 0 and D_FF % tf == 0
    return pl.pallas_call(
        functools.partial(_ffn_body, tf=tf),
        grid=(t // tm,),
        in_specs=[pl.BlockSpec((tm, D_MODEL), lambda i: (i, 0)),
                  _resident((1, D_MODEL)), _resident((1, D_MODEL)),
                  _resident((D_MODEL, 2 * D_FF)), _resident((D_FF, D_MODEL))],
        out_specs=pl.BlockSpec((tm, D_MODEL), lambda i: (i, 0)),
        out_shape=jax.ShapeDtypeStruct((t, D_MODEL), F32),
        scratch_shapes=[pltpu.VMEM((tm, D_MODEL), BF16)],
        compiler_params=_params(("parallel",)),
        name="ffn_block",
    )(x, pre_g, post_g, w_in, w_out)


FOURIER_L1 = MXU_DIM
SUBLANES = 8


def _fourier_tables(l):
    l1 = FOURIER_L1
    l2 = l // l1
    g = MXU_DIM // l2
    w = FOURIER_GROUP_WIDTH
    c = np.arange(w)
    ang = 2.0 * np.pi * np.outer(c, c) / w
    cs_w = np.concatenate([np.cos(ang), -np.sin(ang)], axis=0) / math.sqrt(w)
    n1 = np.arange(l1)
    ang1 = 2.0 * np.pi * np.outer(n1, n1) / l1
    cs1 = np.concatenate([np.cos(ang1), np.sin(ang1)], axis=0)
    k1 = np.arange(l1).reshape(l1 // g, 1, g, 1, 1)
    k2 = np.arange(l2).reshape(1, l2, 1, 1, 1)
    n2 = np.arange(l2).reshape(1, 1, 1, 1, l2)
    theta = 2.0 * np.pi * (n2 * k2 / l2 + n2 * k1 / l)
    eye = np.eye(g).reshape(1, 1, g, g, 1)

    def tiled_cols(m):
        m = m.reshape(l1 // g, l2 * g, g, l2 // SUBLANES, SUBLANES)
        return m.transpose(0, 1, 3, 2, 4).reshape(l1 // g, l2 * g, g * l2)

    gc = tiled_cols(np.cos(theta) * eye)
    gs = tiled_cols(np.sin(theta) * eye)
    g2 = np.concatenate([np.concatenate([gc, -gs], axis=2),
                         np.concatenate([gs, gc], axis=2)], axis=1) / math.sqrt(l)
    as_bf16 = lambda a: jnp.asarray(a, dtype=F32).astype(BF16)
    return as_bf16(cs_w), as_bf16(cs1), as_bf16(g2)


def _fourier_proj_body(x_ref, g_ref, w_ref, wdt_ref, o_ref, dt_ref, *, l2):
    h = _rms(x_ref[...], g_ref[...]).astype(BF16)
    dt_ref[...] = jnp.dot(h, wdt_ref[...], preferred_element_type=F32)
    u = jnp.dot(h, w_ref[...], preferred_element_type=F32)
    tm, tn = u.shape
    u = u.reshape(tm // l2, l2 // SUBLANES, SUBLANES, tn)
    for t in range(l2 // SUBLANES):
        ut = u[:, t].reshape(tm // l2 * SUBLANES, tn)
        for s in range(tn // LANES):
            o_ref[0, t, s] = ut[:, s * LANES:(s + 1) * LANES]


def _fourier_proj(x, g, w, w_dt, b, l, *, tm=1024):
    t, c = x.shape[0], w.shape[1]
    l1 = FOURIER_L1
    l2 = l // l1
    per_seq = l // tm
    assert t == b * l and l % tm == 0 and tm % l2 == 0 and l2 % SUBLANES == 0 and c % LANES == 0
    return pl.pallas_call(
        functools.partial(_fourier_proj_body, l2=l2),
        grid=(t // tm,),
        in_specs=[pl.BlockSpec((tm, D_MODEL), lambda i: (i, 0)),
                  _resident((1, D_MODEL)), _resident((D_MODEL, c)), _resident((D_MODEL, DT_PAD))],
        out_specs=[pl.BlockSpec((1, l2 // SUBLANES, c // LANES, tm // l2 * SUBLANES, LANES),
                                lambda i: (i // per_seq, 0, 0, i % per_seq, 0)),
                   pl.BlockSpec((tm, DT_PAD), lambda i: (i, 0))],
        out_shape=[jax.ShapeDtypeStruct((b, l2 // SUBLANES, c // LANES, l1 * SUBLANES, LANES), F32),
                   jax.ShapeDtypeStruct((t, DT_PAD), F32)],
        compiler_params=_params(("parallel",)),
        name="fourier_proj",
    )(x, g, w, w_dt)


def _pack_bf16_pair(hi, lo):
    bits = lambda v: lax.bitcast_convert_type(v.astype(BF16).astype(F32), jnp.uint32)
    return bits(hi) | (bits(lo) >> 16)


def _unpack_bf16_pair(word):
    hi = lax.bitcast_convert_type(word & jnp.uint32(0xFFFF0000), F32).astype(BF16)
    lo = lax.bitcast_convert_type(word << 16, F32).astype(BF16)
    return hi, lo


def _fourier1_body(u_ref, cs1_ref, z_ref):
    l1 = FOURIER_L1
    slabs = u_ref.shape[0]
    for j in range(SUBLANES):
        pick = pl.ds(j, l1, stride=SUBLANES)
        u = jnp.concatenate([u_ref.at[s][pick, :] for s in range(slabs)], axis=1).astype(BF16)
        z = jnp.dot(cs1_ref[...], u, preferred_element_type=F32)
        word = _pack_bf16_pair(z[:l1], z[l1:])
        for s in range(slabs):
            z_ref.at[s][pick, :] = word[:, s * LANES:(s + 1) * LANES]


def _fourier2_body(z_ref, g2_ref, csw_ref, o_ref, *, tiles, g):
    w = FOURIER_GROUP_WIDTH
    half = MXU_DIM
    slabs = z_ref.shape[2]
    rows = g * SUBLANES
    for t in range(tiles):
        parts = [_unpack_bf16_pair(z_ref[0, :, s, t * rows:(t + 1) * rows, :].reshape(half, LANES))
                 for s in range(slabs)]
        z = jnp.concatenate([jnp.concatenate([p[0] for p in parts], axis=1),
                             jnp.concatenate([p[1] for p in parts], axis=1)], axis=0)
        v = jnp.dot(g2_ref[t], z, preferred_element_type=F32).astype(BF16)
        for s in range(z.shape[1] // w):
            cols = slice(s * w, (s + 1) * w)
            vv = jnp.concatenate([v[:half, cols], v[half:, cols]], axis=1)
            res = jnp.dot(vv, csw_ref[...], preferred_element_type=F32)
            o_ref[0, :, t * g:(t + 1) * g, cols] = res.reshape(o_ref.shape[1], g, w)


def _fourier_branch(u, l, *, slabs_per_step=4, tiles_per_step=2):
    b, nt, ns, rows, _ = u.shape
    c = ns * LANES
    l1 = FOURIER_L1
    l2 = l // l1
    g = MXU_DIM // l2
    assert nt * SUBLANES == l2 and rows == l1 * SUBLANES and MXU_DIM % l2 == 0 and g % SUBLANES == 0
    assert ns % slabs_per_step == 0
    csw, cs1, g2 = _fourier_tables(l)
    blk = pl.BlockSpec((None, None, slabs_per_step, rows, LANES), lambda i, j, k: (i, j, k, 0, 0))
    z = pl.pallas_call(
        _fourier1_body,
        grid=(b, nt, ns // slabs_per_step),
        in_specs=[blk, _resident(cs1.shape)],
        out_specs=blk,
        out_shape=jax.ShapeDtypeStruct(u.shape, jnp.uint32),
        compiler_params=_params(("parallel", "parallel", "parallel")),
        name="fourier_stage1",
    )(u, cs1)
    tiles = tiles_per_step
    assert (l1 // g) % tiles == 0
    y = pl.pallas_call(
        functools.partial(_fourier2_body, tiles=tiles, g=g),
        grid=(b, l1 // g // tiles),
        in_specs=[pl.BlockSpec((1, nt, ns, tiles * g * SUBLANES, LANES), lambda i, t: (i, 0, 0, t, 0)),
                  pl.BlockSpec((tiles, 2 * MXU_DIM, 2 * MXU_DIM), lambda i, t: (t, 0, 0)),
                  _resident(csw.shape)],
        out_specs=pl.BlockSpec((1, l2, tiles * g, c), lambda i, t: (i, 0, t, 0)),
        out_shape=jax.ShapeDtypeStruct((b, l2, l1, c), F32),
        compiler_params=_params(("parallel", "parallel")),
        name="fourier_stage2",
    )(z, g2, csw)
    return y.reshape(b, l, c)


CONV_HALO = BF16_SUBLANES
CONV_PHASES = 4


def _proj_main_body(xp_ref, x_ref, xn_ref, g_ref, w_ref, cw_ref, cb_ref, o_ref, h_ref, ext_ref, act_ref,
                    *, tm, tiles_per_seq):
    i, j = pl.program_id(0), pl.program_id(1)
    halo = CONV_HALO

    @pl.when(j == 0)
    def _():
        h_ref[0:halo, :] = _rms(xp_ref[...], g_ref[...]).astype(BF16)
        h_ref[halo:halo + tm, :] = _rms(x_ref[...], g_ref[...]).astype(BF16)
        h_ref[halo + tm:, :] = _rms(xn_ref[...], g_ref[...]).astype(BF16)

    seq_first = (i % tiles_per_seq) == 0
    seq_last = (i % tiles_per_seq) == tiles_per_seq - 1
    prod = jnp.dot(h_ref[...], w_ref[:, P_TILE:], preferred_element_type=F32)
    n_slabs = P_TILE // LANES
    for s in range(n_slabs):
        ext_ref[s] = prod[:, s * LANES:(s + 1) * LANES]
    ext_ref[:, 0:halo, :] = jnp.where(seq_first, 0.0, ext_ref[:, 0:halo, :])
    ext_ref[:, halo + tm:, :] = jnp.where(seq_last, 0.0, ext_ref[:, halo + tm:, :])

    o_ref[:, :P_TILE] = jnp.dot(h_ref[halo:halo + tm, :], w_ref[:, :P_TILE],
                                preferred_element_type=F32).astype(o_ref.dtype)
    n = tm // CONV_PHASES
    for s in range(n_slabs):
        lanes = slice(s * LANES, (s + 1) * LANES)
        ext, act = ext_ref.at[s], act_ref.at[s]
        taps = {m: ext[pl.ds(halo + m, n, stride=CONV_PHASES), :]
                for m in range(-CONV_PAD, CONV_PHASES + CONV_PAD)}
        for j in range(CONV_PHASES):
            acc = cb_ref[:, lanes]
            for k in range(CONV_WIDTH):
                acc = acc + cw_ref[k:k + 1, lanes] * taps[j + k - CONV_PAD]
            act[pl.ds(j, n, stride=CONV_PHASES), :] = acc * jax.nn.sigmoid(acc)
    for s in range(n_slabs):
        o_ref[:, P_TILE + s * LANES:P_TILE + (s + 1) * LANES] = act_ref[s].astype(o_ref.dtype)


def _proj_main(x, g, w, conv_w, conv_b, l, *, tm=1024):
    t, n = x.shape[0], w.shape[1]
    halo = CONV_HALO
    pair = 2 * P_TILE
    assert t % tm == 0 and l % tm == 0 and n == P_MAIN and tm % halo == 0
    assert P_PLAIN == tuple(range(0, 2 * len(P_PLAIN), 2)) and P_CONV == tuple(range(1, 2 * len(P_CONV), 2))
    hb = tm // halo
    nhb = t // halo
    return pl.pallas_call(
        functools.partial(_proj_main_body, tm=tm, tiles_per_seq=l // tm),
        grid=(t // tm, n // pair),
        in_specs=[pl.BlockSpec((halo, D_MODEL), lambda i, j: (jnp.maximum(i * hb - 1, 0), 0)),
                  pl.BlockSpec((tm, D_MODEL), lambda i, j: (i, 0)),
                  pl.BlockSpec((halo, D_MODEL), lambda i, j: (jnp.minimum((i + 1) * hb, nhb - 1), 0)),
                  pl.BlockSpec((1, D_MODEL), lambda i, j: (0, 0)),
                  pl.BlockSpec((D_MODEL, pair), lambda i, j: (0, j)),
                  pl.BlockSpec((CONV_WIDTH, P_TILE), lambda i, j: (0, j)),
                  pl.BlockSpec((1, P_TILE), lambda i, j: (0, j))],
        out_specs=pl.BlockSpec((tm, pair), lambda i, j: (i, j)),
        out_shape=jax.ShapeDtypeStruct((t, n), BF16),
        scratch_shapes=[pltpu.VMEM((tm + 2 * halo, D_MODEL), BF16),
                        pltpu.VMEM((P_TILE // LANES, tm + 2 * halo, LANES), F32),
                        pltpu.VMEM((P_TILE // LANES, tm, LANES), F32)],
        compiler_params=_params(("parallel", "arbitrary")),
        name="proj_main",
    )(x, x, x, g, w, conv_w, conv_b)


LOG2E = math.log2(math.e)

HEADS_PER_PAIR = LANES // SSM_HEAD_DIM
N_PAIRS = SSM_HEADS // HEADS_PER_PAIR
HEADS_PER_GROUP = SSM_HEADS // SSM_GROUPS
PAIRS_PER_GROUP = HEADS_PER_GROUP // HEADS_PER_PAIR


def _split3(a):
    hi = a.astype(BF16)
    r1 = a - hi.astype(F32)
    mid = r1.astype(BF16)
    lo = (r1 - mid.astype(F32)).astype(BF16)
    return hi, mid, lo


def _ssd_prep_body(dt_ref, bias_ref, alog_ref, cum_ref, rowb_ref, w_ref, *, chunks):
    q = CHUNK
    row = lax.broadcasted_iota(jnp.int32, (q, q), 0)
    col = lax.broadcasted_iota(jnp.int32, (q, q), 1)
    tri = jnp.concatenate([jnp.where(col <= row, 1.0, 0.0), jnp.where(col >= row, 1.0, 0.0)],
                          axis=0).astype(BF16)
    fwd_lane = lax.broadcasted_iota(jnp.int32, (q, LANES), 1) < SSM_HEADS
    neg_a = -jnp.exp(alog_ref[...]) * LOG2E
    for c in range(chunks):
        rows = slice(c * q, (c + 1) * q)
        dtv = jax.nn.softplus(dt_ref[0, rows, :] + bias_ref[...])
        a = dtv * neg_a
        both = sum(jnp.dot(tri, piece, preferred_element_type=F32) for piece in _split3(a))
        cum2 = jnp.where(fwd_lane, both[:q], both[q:])
        cend = jnp.where(fwd_lane[0:1], cum2[q - 1:q], cum2[0:1])
        cum_ref[0, rows, :] = cum2
        rowb_ref[0, rows, :] = (cum2 - jnp.log2(dtv)).T
        w_ref[0, rows, :] = (jnp.exp2(cend - cum2) * dtv).T


def _ssd_prep(dt_raw, bias, alog, *, chunks=4):
    b, l, _ = dt_raw.shape
    rows = chunks * CHUNK
    assert l % rows == 0
    spec = pl.BlockSpec((1, rows, DT_PAD), lambda n, c: (n, c, 0))
    vec = pl.BlockSpec((1, DT_PAD), lambda n, c: (0, 0))
    shape = jax.ShapeDtypeStruct((b, l, DT_PAD), F32)
    return pl.pallas_call(
        functools.partial(_ssd_prep_body, chunks=chunks),
        grid=(b, l // rows),
        in_specs=[spec, vec, vec],
        out_specs=[spec, spec, spec],
        out_shape=[shape, shape, shape],
        compiler_params=_params(("parallel", "parallel")),
        name="ssd_prep",
    )(dt_raw, bias, alog)


def _ssd_body(*refs, direction, final, cps):
    if final:
        (x0_ref, x1_ref, b_ref, c_ref, cum_ref, rowb_ref, w_ref, yf_ref, z0_ref, z1_ref, dskip_ref,
         ng_ref, o_ref, s_ref, yacc_ref) = refs
    else:
        x0_ref, x1_ref, b_ref, c_ref, cum_ref, rowb_ref, w_ref, o_ref, s_ref = refs
    q = CHUNK
    pairs_per_tile = P_TILE // LANES

    @pl.when(pl.program_id(1) == 0)
    def _():
        s_ref[...] = jnp.zeros_like(s_ref)

    row = lax.broadcasted_iota(jnp.int32, (q, q), 0)
    col = lax.broadcasted_iota(jnp.int32, (q, q), 1)
    causal = (col <= row) if direction == 0 else (col >= row)
    end = q - 1 if direction == 0 else 0
    first_head = lax.broadcasted_iota(jnp.int32, (1, LANES), 1) < SSM_HEAD_DIM
    keep_bf = [jnp.where(first_head, 1.0, 0.0).astype(BF16),
               jnp.where(first_head, 0.0, 1.0).astype(BF16)]

    def group_prologue(grp, ck):
        gcols = slice(grp * D_STATE, (grp + 1) * D_STATE)
        rows = slice(ck * q, (ck + 1) * q)
        bg = b_ref[0, rows, gcols]
        cg = c_ref[0, rows, gcols]
        cb = lax.dot_general(cg, bg, (((1,), (1,)), ((), ())), preferred_element_type=F32)
        bg_t = bg.astype(F32).T
        first_pair = grp * PAIRS_PER_GROUP
        s_grp = jnp.concatenate([s_ref[first_pair + k].astype(BF16) for k in range(PAIRS_PER_GROUP)], axis=1)
        y_off = jnp.dot(cg, s_grp, preferred_element_type=F32)
        return cb, bg_t, y_off

    def pair_work(grp, ck, k, cb, bg_t, y_off):
        r0 = ck * q
        rows = slice(r0, r0 + q)
        pr = grp * PAIRS_PER_GROUP + k
        x_ref = x0_ref if pr < pairs_per_tile else x1_ref
        xp = x_ref[0, rows, (pr % pairs_per_tile) * LANES:(pr % pairs_per_tile + 1) * LANES]
        ms, bws, xs, ecols, cds = [], [], [], [], []
        for i in range(HEADS_PER_PAIR):
            hc = SSM_HEADS * direction + HEADS_PER_PAIR * pr + i
            colb = jnp.broadcast_to(cum_ref[0, rows, hc:hc + 1], (q, LANES))
            m = jnp.where(causal, jnp.exp2(colb - rowb_ref[0, r0 + hc:r0 + hc + 1, :]), 0.0) * cb
            ms.append(m.astype(BF16))
            xs.append(xp * keep_bf[i])
            bws.append((bg_t * w_ref[0, r0 + hc:r0 + hc + 1, :]).astype(BF16))
            ecols.append(jnp.exp2(colb))
            cds.append(ecols[i][end:end + 1, :])
        x_split = jnp.concatenate(xs, axis=0)
        y_diag = jnp.dot(jnp.concatenate(ms, axis=1), x_split, preferred_element_type=F32)
        y_pair = y_diag + y_off[:, k * LANES:(k + 1) * LANES] * jnp.where(first_head, ecols[0], ecols[1])
        s_new = jnp.dot(jnp.concatenate(bws, axis=1), x_split, preferred_element_type=F32)
        s_ref[pr] = s_ref[pr] * jnp.where(first_head, cds[0], cds[1]) + s_new
        if final:
            yacc_ref[rows, pr * LANES:(pr + 1) * LANES] = y_pair
        else:
            o_ref[0, rows, pr * LANES:(pr + 1) * LANES] = y_pair.astype(o_ref.dtype)

    order = range(cps) if direction == 0 else range(cps - 1, -1, -1)
    for g0 in range(0, SSM_GROUPS, 2):
        for ck in order:
            pro = [group_prologue(g0 + d, ck) for d in range(2)]
            for k in range(PAIRS_PER_GROUP):
                for d in range(2):
                    pair_work(g0 + d, ck, k, *pro[d])

    if final:
        x = jnp.concatenate([x0_ref[0], x1_ref[0]], axis=1).astype(F32)
        y = yacc_ref[...] + yf_ref[0].astype(F32) + x * dskip_ref[...]
        z = jnp.concatenate([z0_ref[0], z1_ref[0]], axis=1).astype(F32)
        o_ref[0] = _rms(y * (z * jax.nn.sigmoid(z)), ng_ref[...]).astype(o_ref.dtype)


def _ssd_pass(p_main, prep, direction, final_inputs=None, *, cps=4):
    b, l, _ = p_main.shape
    rows = cps * CHUNK
    nc = l // rows
    assert l % rows == 0 and D_INNER == 2 * P_TILE and SSM_GROUPS * D_STATE == P_TILE
    chunk = (lambda c: c) if direction == 0 else (lambda c: nc - 1 - c)
    tile = lambda k: pl.BlockSpec((1, rows, P_TILE), lambda n, c: (n, chunk(c), k))
    small = pl.BlockSpec((1, rows, DT_PAD), lambda n, c: (n, chunk(c), 0))
    in_specs = [tile(P_CONV[0]), tile(P_CONV[1]), tile(P_CONV[2]), tile(P_CONV[3]), small, small, small]
    args = [p_main, p_main, p_main, p_main, *prep]
    scratch = [pltpu.VMEM((N_PAIRS, D_STATE, LANES), F32)]
    final = final_inputs is not None
    if final:
        y_fwd, dskip, norm_g = final_inputs
        in_specs += [pl.BlockSpec((1, rows, D_INNER), lambda n, c: (n, chunk(c), 0)),
                     tile(P_PLAIN[0]), tile(P_PLAIN[1]),
                     pl.BlockSpec((1, D_INNER), lambda n, c: (0, 0)),
                     pl.BlockSpec((1, D_INNER), lambda n, c: (0, 0))]
        args += [y_fwd, p_main, p_main, dskip, norm_g]
        scratch.append(pltpu.VMEM((rows, D_INNER), F32))
    return pl.pallas_call(
        functools.partial(_ssd_body, direction=direction, final=final, cps=cps),
        grid=(b, nc),
        in_specs=in_specs,
        out_specs=pl.BlockSpec((1, rows, D_INNER), lambda n, c: (n, chunk(c), 0)),
        out_shape=jax.ShapeDtypeStruct((b, l, D_INNER), BF16),
        scratch_shapes=scratch,
        compiler_params=_params(("parallel", "arbitrary")),
        name="ssd_bwd_norm" if final else "ssd_fwd",
    )(*args)


def _merge_body(x_ref, yf_ref, ym_ref, gf_ref, gm_ref, wf_ref, wm_ref, wo_ref, post_ref, o_ref):
    f = jnp.dot(yf_ref[...].astype(BF16), wf_ref[...], preferred_element_type=F32)
    m = jnp.dot(ym_ref[...], wm_ref[...], preferred_element_type=F32)
    merged = (jax.nn.sigmoid(gf_ref[...].astype(F32)) * f
              + jax.nn.sigmoid(gm_ref[...].astype(F32)) * m).astype(BF16)
    out = jnp.dot(merged, wo_ref[...], preferred_element_type=F32)
    o_ref[...] = x_ref[...] + _rms(out, post_ref[...])


def _merge_block(x, yf, ym, p_main, w_f, w_m, w_o, post_g, *, tm=512):
    t = x.shape[0]
    assert t % tm == 0
    assert P_TILE == D_MODEL
    return pl.pallas_call(
        _merge_body,
        grid=(t // tm,),
        in_specs=[pl.BlockSpec((tm, D_MODEL), lambda i: (i, 0)),
                  pl.BlockSpec((tm, D_FOURIER), lambda i: (i, 0)),
                  pl.BlockSpec((tm, D_INNER), lambda i: (i, 0)),
                  pl.BlockSpec((tm, D_MODEL), lambda i: (i, P_PLAIN[2])),
                  pl.BlockSpec((tm, D_MODEL), lambda i: (i, P_PLAIN[3])),
                  _resident((D_FOURIER, D_MODEL)), _resident((D_INNER, D_MODEL)),
                  _resident((D_MODEL, D_MODEL)), _resident((1, D_MODEL))],
        out_specs=pl.BlockSpec((tm, D_MODEL), lambda i: (i, 0)),
        out_shape=jax.ShapeDtypeStruct((t, D_MODEL), F32),
        compiler_params=_params(("parallel",)),
        name="merge_out",
    )(x, yf, ym, p_main, p_main, w_f, w_m, w_o, post_g)


def _prepare(ffn1_pre_g, ffn1_post_g, ffn1_w_in, ffn1_w_out, mix_pre_g, mix_post_g, w_in, conv_w,
             conv_b, dt_bias, a_log, d_skip, ssm_norm_g, w_branch_f, w_branch_m, w_out,
             ffn2_pre_g, ffn2_post_g, ffn2_w_in, ffn2_w_out):
    row = lambda v: v.reshape(1, -1).astype(F32)
    off_f, off_z = D_FOURIER, D_FOURIER + D_INNER
    off_xbc = off_z + CONV_DIM
    off_dt = off_xbc + 2 * SSM_HEADS
    plain = jnp.concatenate([w_in[:, off_f:off_z], w_in[:, off_dt:]], axis=1)
    conv = w_in[:, off_z:off_xbc]
    w_main = jnp.concatenate([part[:, k * P_TILE:(k + 1) * P_TILE]
                              for k in range(len(P_PLAIN)) for part in (plain, conv)], axis=1).astype(BF16)
    pad = DT_PAD - 2 * SSM_HEADS
    w_dt = jnp.pad(w_in[:, off_xbc:off_dt], ((0, 0), (0, pad))).astype(BF16)
    return dict(
        ffn1=(row(ffn1_pre_g), row(ffn1_post_g), ffn1_w_in.astype(BF16), ffn1_w_out.astype(BF16)),
        ffn2=(row(ffn2_pre_g), row(ffn2_post_g), ffn2_w_in.astype(BF16), ffn2_w_out.astype(BF16)),
        mix_pre=row(mix_pre_g), mix_post=row(mix_post_g),
        w_main=w_main, w_four=w_in[:, :off_f].astype(BF16), w_dt=w_dt,
        conv_w=conv_w.astype(F32), conv_b=row(conv_b),
        dt_bias=jnp.pad(row(dt_bias), ((0, 0), (0, pad))),
        a_log=jnp.pad(row(a_log), ((0, 0), (0, pad))),
        d_skip=row(jnp.repeat(d_skip, SSM_HEAD_DIM)), norm_g=row(ssm_norm_g),
        w_f=w_branch_f.astype(BF16), w_m=w_branch_m.astype(BF16), w_o=w_out.astype(BF16))


def _encoder_layer(x, p):
    b, l, d = x.shape
    t = b * l
    x = _ffn_block(x.reshape(t, d), *p["ffn1"])
    p_main = _proj_main(x, p["mix_pre"], p["w_main"], p["conv_w"], p["conv_b"], l).reshape(b, l, P_MAIN)
    u_f, dt_raw = _fourier_proj(x, p["mix_pre"], p["w_four"], p["w_dt"], b, l)
    dt_raw = dt_raw.reshape(b, l, DT_PAD)
    yf = _fourier_branch(u_f, l)
    prep = _ssd_prep(dt_raw, p["dt_bias"], p["a_log"])
    y_fwd = _ssd_pass(p_main, prep, 0)
    ym = _ssd_pass(p_main, prep, 1, final_inputs=(y_fwd, p["d_skip"], p["norm_g"]))
    x = _merge_block(x, yf.reshape(t, D_FOURIER), ym.reshape(t, D_INNER), p_main.reshape(t, P_MAIN),
                     p["w_f"], p["w_m"], p["w_o"], p["mix_post"])
    x = _ffn_block(x, *p["ffn2"])
    return x.reshape(b, l, d)


def kernel(x_prompt, x_sample, ffn1_pre_g, ffn1_post_g, ffn1_w_in, ffn1_w_out, mix_pre_g, mix_post_g, w_in, conv_w, conv_b, dt_bias, a_log, d_skip, ssm_norm_g, w_branch_f, w_branch_m, w_out, ffn2_pre_g, ffn2_post_g, ffn2_w_in, ffn2_w_out):
    layers = (ffn1_pre_g, ffn1_post_g, ffn1_w_in, ffn1_w_out, mix_pre_g, mix_post_g, w_in, conv_w,
              conv_b, dt_bias, a_log, d_skip, ssm_norm_g, w_branch_f, w_branch_m, w_out,
              ffn2_pre_g, ffn2_post_g, ffn2_w_in, ffn2_w_out)
    y_prompt, y_sample = x_prompt, x_sample
    for layer in range(ffn1_pre_g.shape[0]):
        p = _prepare(*(a[layer] for a in layers))
        y_prompt = _encoder_layer(y_prompt, p)
        y_sample = _encoder_layer(y_sample, p)
    return (y_prompt, y_sample)
```

```python
import functools
import math

import numpy as np
import jax
import jax.numpy as jnp
from jax import lax
from jax.experimental import pallas as pl
from jax.experimental.pallas import tpu as pltpu

F32 = jnp.float32
BF16 = jnp.bfloat16

D_MODEL = 1024
D_FF = 2816
FOURIER_GROUP_WIDTH = 256
D_FOURIER = 1024
SSM_HEADS = 32
SSM_HEAD_DIM = 64
D_INNER = SSM_HEADS * SSM_HEAD_DIM
SSM_GROUPS = 8
D_STATE = 128
CONV_WIDTH = 5
CONV_PAD = CONV_WIDTH // 2
CHUNK = 128
CONV_DIM = D_INNER + 2 * SSM_GROUPS * D_STATE
EPS = 1e-6

LANES = 128
MXU_DIM = 256
BF16_SUBLANES = 16
VMEM_LIMIT = 56 * 1024 * 1024

P_TILE = 1024
P_PLAIN = (0, 2, 4, 6)
P_CONV = (1, 3, 5, 7)
P_MAIN = (len(P_PLAIN) + len(P_CONV)) * P_TILE
DT_PAD = LANES


def _params(sem):
    return pltpu.CompilerParams(dimension_semantics=sem, vmem_limit_bytes=VMEM_LIMIT)


def _rms(x, g):
    return x * lax.rsqrt(jnp.mean(x * x, axis=-1, keepdims=True) + EPS) * g


def _resident(shape):
    nd = len(shape)
    return pl.BlockSpec(shape, lambda *_: (0,) * nd, pipeline_mode=pl.Buffered(1))


def _ffn_body(x_ref, pre_ref, post_ref, win_ref, wout_ref, o_ref, h_ref, *, tf):
    x = x_ref[...]
    h_ref[...] = _rms(x, pre_ref[...]).astype(BF16)
    acc = None
    for c in range(D_FF // tf):
        h = h_ref[...]
        gate = jnp.dot(h, win_ref[:, c * tf:(c + 1) * tf], preferred_element_type=F32)
        up = jnp.dot(h, win_ref[:, D_FF + c * tf:D_FF + (c + 1) * tf], preferred_element_type=F32)
        act = (gate * jax.nn.sigmoid(gate) * up).astype(BF16)
        part = jnp.dot(act, wout_ref[c * tf:(c + 1) * tf, :], preferred_element_type=F32)
        acc = part if acc is None else acc + part
    o_ref[...] = x + 0.5 * _rms(acc, post_ref[...])


def _ffn_block(x, pre_g, post_g, w_in, w_out, *, tm=512, tf=256):
    t = x.shape[0]
    assert t % tm == 0 and D_FF % tf == 0
    return pl.pallas_call(
        functools.partial(_ffn_body, tf=tf),
        grid=(t // tm,),
        in_specs=[pl.BlockSpec((tm, D_MODEL), lambda i: (i, 0)),
                  _resident((1, D_MODEL)), _resident((1, D_MODEL)),
                  _resident((D_MODEL, 2 * D_FF)), _resident((D_FF, D_MODEL))],
        out_specs=pl.BlockSpec((tm, D_MODEL), lambda i: (i, 0)),
        out_shape=jax.ShapeDtypeStruct((t, D_MODEL), F32),
        scratch_shapes=[pltpu.VMEM((tm, D_MODEL), BF16)],
        compiler_params=_params(("parallel",)),
        name="ffn_block",
    )(x, pre_g, post_g, w_in, w_out)


FOURIER_L1 = MXU_DIM
SUBLANES = 8


def _fourier_tables(l):
    l1 = FOURIER_L1
    l2 = l // l1
    g = MXU_DIM // l2
    w = FOURIER_GROUP_WIDTH
    c = np.arange(w)
    ang = 2.0 * np.pi * np.outer(c, c) / w
    cs_w = np.concatenate([np.cos(ang), -np.sin(ang)], axis=0) / math.sqrt(w)
    n1 = np.arange(l1)
    ang1 = 2.0 * np.pi * np.outer(n1, n1) / l1
    cs1 = np.concatenate([np.cos(ang1), np.sin(ang1)], axis=0)
    k1 = np.arange(l1).reshape(l1 // g, 1, g, 1, 1)
    k2 = np.arange(l2).reshape(1, l2, 1, 1, 1)
    n2 = np.arange(l2).reshape(1, 1, 1, 1, l2)
    theta = 2.0 * np.pi * (n2 * k2 / l2 + n2 * k1 / l)
    eye = np.eye(g).reshape(1, 1, g, g, 1)

    def tiled_cols(m):
        m = m.reshape(l1 // g, l2 * g, g, l2 // SUBLANES, SUBLANES)
        return m.transpose(0, 1, 3, 2, 4).reshape(l1 // g, l2 * g, g * l2)

    gc = tiled_cols(np.cos(theta) * eye)
    gs = tiled_cols(np.sin(theta) * eye)
    g2 = np.concatenate([np.concatenate([gc, -gs], axis=2),
                         np.concatenate([gs, gc], axis=2)], axis=1) / math.sqrt(l)
    as_bf16 = lambda a: jnp.asarray(a, dtype=F32).astype(BF16)
    return as_bf16(cs_w), as_bf16(cs1), as_bf16(g2)


def _fourier_proj_body(x_ref, g_ref, w_ref, wdt_ref, o_ref, dt_ref, *, l2):
    h = _rms(x_ref[...], g_ref[...]).astype(BF16)
    dt_ref[...] = jnp.dot(h, wdt_ref[...], preferred_element_type=F32)
    u = jnp.dot(h, w_ref[...], preferred_element_type=F32)
    tm, tn = u.shape
    u = u.reshape(tm // l2, l2 // SUBLANES, SUBLANES, tn)
    for t in range(l2 // SUBLANES):
        ut = u[:, t].reshape(tm // l2 * SUBLANES, tn)
        for s in range(tn // LANES):
            o_ref[0, t, s] = ut[:, s * LANES:(s + 1) * LANES]


def _fourier_proj(x, g, w, w_dt, b, l, *, tm=1024):
    t, c = x.shape[0], w.shape[1]
    l1 = FOURIER_L1
    l2 = l // l1
    per_seq = l // tm
    assert t == b * l and l % tm == 0 and tm % l2 == 0 and l2 % SUBLANES == 0 and c % LANES == 0
    return pl.pallas_call(
        functools.partial(_fourier_proj_body, l2=l2),
        grid=(t // tm,),
        in_specs=[pl.BlockSpec((tm, D_MODEL), lambda i: (i, 0)),
                  _resident((1, D_MODEL)), _resident((D_MODEL, c)), _resident((D_MODEL, DT_PAD))],
        out_specs=[pl.BlockSpec((1, l2 // SUBLANES, c // LANES, tm // l2 * SUBLANES, LANES),
                                lambda i: (i // per_seq, 0, 0, i % per_seq, 0)),
                   pl.BlockSpec((tm, DT_PAD), lambda i: (i, 0))],
        out_shape=[jax.ShapeDtypeStruct((b, l2 // SUBLANES, c // LANES, l1 * SUBLANES, LANES), F32),
                   jax.ShapeDtypeStruct((t, DT_PAD), F32)],
        compiler_params=_params(("parallel",)),
        name="fourier_proj",
    )(x, g, w, w_dt)


def _pack_bf16_pair(hi, lo):
    bits = lambda v: lax.bitcast_convert_type(v.astype(BF16).astype(F32), jnp.uint32)
    return bits(hi) | (bits(lo) >> 16)


def _unpack_bf16_pair(word):
    hi = lax.bitcast_convert_type(word & jnp.uint32(0xFFFF0000), F32).astype(BF16)
    lo = lax.bitcast_convert_type(word << 16, F32).astype(BF16)
    return hi, lo


def _fourier1_body(u_ref, cs1_ref, z_ref):
    l1 = FOURIER_L1
    slabs = u_ref.shape[0]
    for j in range(SUBLANES):
        pick = pl.ds(j, l1, stride=SUBLANES)
        u = jnp.concatenate([u_ref.at[s][pick, :] for s in range(slabs)], axis=1).astype(BF16)
        z = jnp.dot(cs1_ref[...], u, preferred_element_type=F32)
        word = _pack_bf16_pair(z[:l1], z[l1:])
        for s in range(slabs):
            z_ref.at[s][pick, :] = word[:, s * LANES:(s + 1) * LANES]


def _fourier2_body(z_ref, g2_ref, csw_ref, o_ref, *, tiles, g):
    w = FOURIER_GROUP_WIDTH
    half = MXU_DIM
    slabs = z_ref.shape[2]
    rows = g * SUBLANES
    for t in range(tiles):
        parts = [_unpack_bf16_pair(z_ref[0, :, s, t * rows:(t + 1) * rows, :].reshape(half, LANES))
                 for s in range(slabs)]
        z = jnp.concatenate([jnp.concatenate([p[0] for p in parts], axis=1),
                             jnp.concatenate([p[1] for p in parts], axis=1)], axis=0)
        v = jnp.dot(g2_ref[t], z, preferred_element_type=F32).astype(BF16)
        for s in range(z.shape[1] // w):
            cols = slice(s * w, (s + 1) * w)
            vv = jnp.concatenate([v[:half, cols], v[half:, cols]], axis=1)
            res = jnp.dot(vv, csw_ref[...], preferred_element_type=F32)
            o_ref[0, :, t * g:(t + 1) * g, cols] = res.reshape(o_ref.shape[1], g, w)


def _fourier_branch(u, l, *, slabs_per_step=4, tiles_per_step=2):
    b, nt, ns, rows, _ = u.shape
    c = ns * LANES
    l1 = FOURIER_L1
    l2 = l // l1
    g = MXU_DIM // l2
    assert nt * SUBLANES == l2 and rows == l1 * SUBLANES and MXU_DIM % l2 == 0 and g % SUBLANES == 0
    assert ns % slabs_per_step == 0
    csw, cs1, g2 = _fourier_tables(l)
    blk = pl.BlockSpec((None, None, slabs_per_step, rows, LANES), lambda i, j, k: (i, j, k, 0, 0))
    z = pl.pallas_call(
        _fourier1_body,
        grid=(b, nt, ns // slabs_per_step),
        in_specs=[blk, _resident(cs1.shape)],
        out_specs=blk,
        out_shape=jax.ShapeDtypeStruct(u.shape, jnp.uint32),
        compiler_params=_params(("parallel", "parallel", "parallel")),
        name="fourier_stage1",
    )(u, cs1)
    tiles = tiles_per_step
    assert (l1 // g) % tiles == 0
    y = pl.pallas_call(
        functools.partial(_fourier2_body, tiles=tiles, g=g),
        grid=(b, l1 // g // tiles),
        in_specs=[pl.BlockSpec((1, nt, ns, tiles * g * SUBLANES, LANES), lambda i, t: (i, 0, 0, t, 0)),
                  pl.BlockSpec((tiles, 2 * MXU_DIM, 2 * MXU_DIM), lambda i, t: (t, 0, 0)),
                  _resident(csw.shape)],
        out_specs=pl.BlockSpec((1, l2, tiles * g, c), lambda i, t: (i, 0, t, 0)),
        out_shape=jax.ShapeDtypeStruct((b, l2, l1, c), F32),
        compiler_params=_params(("parallel", "parallel")),
        name="fourier_stage2",
    )(z, g2, csw)
    return y.reshape(b, l, c)


CONV_HALO = BF16_SUBLANES
CONV_PHASES = 4


def _proj_main_body(xp_ref, x_ref, xn_ref, g_ref, w_ref, cw_ref, cb_ref, o_ref, h_ref, ext_ref, act_ref,
                    *, tm, tiles_per_seq):
    i, j = pl.program_id(0), pl.program_id(1)
    halo = CONV_HALO

    @pl.when(j == 0)
    def _():
        h_ref[0:halo, :] = _rms(xp_ref[...], g_ref[...]).astype(BF16)
        h_ref[halo:halo + tm, :] = _rms(x_ref[...], g_ref[...]).astype(BF16)
        h_ref[halo + tm:, :] = _rms(xn_ref[...], g_ref[...]).astype(BF16)

    seq_first = (i % tiles_per_seq) == 0
    seq_last = (i % tiles_per_seq) == tiles_per_seq - 1
    prod = jnp.dot(h_ref[...], w_ref[:, P_TILE:], preferred_element_type=F32)
    n_slabs = P_TILE // LANES
    for s in range(n_slabs):
        ext_ref[s] = prod[:, s * LANES:(s + 1) * LANES]
    ext_ref[:, 0:halo, :] = jnp.where(seq_first, 0.0, ext_ref[:, 0:halo, :])
    ext_ref[:, halo + tm:, :] = jnp.where(seq_last, 0.0, ext_ref[:, halo + tm:, :])

    o_ref[:, :P_TILE] = jnp.dot(h_ref[halo:halo + tm, :], w_ref[:, :P_TILE],
                                preferred_element_type=F32).astype(o_ref.dtype)
    n = tm // CONV_PHASES
    for s in range(n_slabs):
        lanes = slice(s * LANES, (s + 1) * LANES)
        ext, act = ext_ref.at[s], act_ref.at[s]
        taps = {m: ext[pl.ds(halo + m, n, stride=CONV_PHASES), :]
                for m in range(-CONV_PAD, CONV_PHASES + CONV_PAD)}
        for j in range(CONV_PHASES):
            acc = cb_ref[:, lanes]
            for k in range(CONV_WIDTH):
                acc = acc + cw_ref[k:k + 1, lanes] * taps[j + k - CONV_PAD]
            act[pl.ds(j, n, stride=CONV_PHASES), :] = acc * jax.nn.sigmoid(acc)
    for s in range(n_slabs):
        o_ref[:, P_TILE + s * LANES:P_TILE + (s + 1) * LANES] = act_ref[s].astype(o_ref.dtype)


def _proj_main(x, g, w, conv_w, conv_b, l, *, tm=1024):
    t, n = x.shape[0], w.shape[1]
    halo = CONV_HALO
    pair = 2 * P_TILE
    assert t % tm == 0 and l % tm == 0 and n == P_MAIN and tm % halo == 0
    assert P_PLAIN == tuple(range(0, 2 * len(P_PLAIN), 2)) and P_CONV == tuple(range(1, 2 * len(P_CONV), 2))
    hb = tm // halo
    nhb = t // halo
    return pl.pallas_call(
        functools.partial(_proj_main_body, tm=tm, tiles_per_seq=l // tm),
        grid=(t // tm, n // pair),
        in_specs=[pl.BlockSpec((halo, D_MODEL), lambda i, j: (jnp.maximum(i * hb - 1, 0), 0)),
                  pl.BlockSpec((tm, D_MODEL), lambda i, j: (i, 0)),
                  pl.BlockSpec((halo, D_MODEL), lambda i, j: (jnp.minimum((i + 1) * hb, nhb - 1), 0)),
                  pl.BlockSpec((1, D_MODEL), lambda i, j: (0, 0)),
                  pl.BlockSpec((D_MODEL, pair), lambda i, j: (0, j)),
                  pl.BlockSpec((CONV_WIDTH, P_TILE), lambda i, j: (0, j)),
                  pl.BlockSpec((1, P_TILE), lambda i, j: (0, j))],
        out_specs=pl.BlockSpec((tm, pair), lambda i, j: (i, j)),
        out_shape=jax.ShapeDtypeStruct((t, n), BF16),
        scratch_shapes=[pltpu.VMEM((tm + 2 * halo, D_MODEL), BF16),
                        pltpu.VMEM((P_TILE // LANES, tm + 2 * halo, LANES), F32),
                        pltpu.VMEM((P_TILE // LANES, tm, LANES), F32)],
        compiler_params=_params(("parallel", "arbitrary")),
        name="proj_main",
    )(x, x, x, g, w, conv_w, conv_b)


LOG2E = math.log2(math.e)

HEADS_PER_PAIR = LANES // SSM_HEAD_DIM
N_PAIRS = SSM_HEADS // HEADS_PER_PAIR
HEADS_PER_GROUP = SSM_HEADS // SSM_GROUPS
PAIRS_PER_GROUP = HEADS_PER_GROUP // HEADS_PER_PAIR


def _split3(a):
    hi = a.astype(BF16)
    r1 = a - hi.astype(F32)
    mid = r1.astype(BF16)
    lo = (r1 - mid.astype(F32)).astype(BF16)
    return hi, mid, lo


def _ssd_prep_body(dt_ref, bias_ref, alog_ref, cum_ref, rowb_ref, w_ref, *, chunks):
    q = CHUNK
    row = lax.broadcasted_iota(jnp.int32, (q, q), 0)
    col = lax.broadcasted_iota(jnp.int32, (q, q), 1)
    tri = jnp.concatenate([jnp.where(col <= row, 1.0, 0.0), jnp.where(col >= row, 1.0, 0.0)],
                          axis=0).astype(BF16)
    fwd_lane = lax.broadcasted_iota(jnp.int32, (q, LANES), 1) < SSM_HEADS
    neg_a = -jnp.exp(alog_ref[...]) * LOG2E
    for c in range(chunks):
        rows = slice(c * q, (c + 1) * q)
        dtv = jax.nn.softplus(dt_ref[0, rows, :] + bias_ref[...])
        a = dtv * neg_a
        both = sum(jnp.dot(tri, piece, preferred_element_type=F32) for piece in _split3(a))
        cum2 = jnp.where(fwd_lane, both[:q], both[q:])
        cend = jnp.where(fwd_lane[0:1], cum2[q - 1:q], cum2[0:1])
        cum_ref[0, rows, :] = cum2
        rowb_ref[0, rows, :] = (cum2 - jnp.log2(dtv)).T
        w_ref[0, rows, :] = (jnp.exp2(cend - cum2) * dtv).T


def _ssd_prep(dt_raw, bias, alog, *, chunks=4):
    b, l, _ = dt_raw.shape
    rows = chunks * CHUNK
    assert l % rows == 0
    spec = pl.BlockSpec((1, rows, DT_PAD), lambda n, c: (n, c, 0))
    vec = pl.BlockSpec((1, DT_PAD), lambda n, c: (0, 0))
    shape = jax.ShapeDtypeStruct((b, l, DT_PAD), F32)
    return pl.pallas_call(
        functools.partial(_ssd_prep_body, chunks=chunks),
        grid=(b, l // rows),
        in_specs=[spec, vec, vec],
        out_specs=[spec, spec, spec],
        out_shape=[shape, shape, shape],
        compiler_params=_params(("parallel", "parallel")),
        name="ssd_prep",
    )(dt_raw, bias, alog)


def _ssd_body(*refs, direction, final, cps):
    if final:
        x0_ref, x1_ref, b_ref, c_ref, cum_ref, rowb_ref, w_ref, yf_ref, dskip_ref, o_ref, s_ref = refs
    else:
        x0_ref, x1_ref, b_ref, c_ref, cum_ref, rowb_ref, w_ref, o_ref, s_ref = refs
    q = CHUNK
    pairs_per_tile = P_TILE // LANES

    @pl.when(pl.program_id(1) == 0)
    def _():
        s_ref[...] = jnp.zeros_like(s_ref)

    row = lax.broadcasted_iota(jnp.int32, (q, q), 0)
    col = lax.broadcasted_iota(jnp.int32, (q, q), 1)
    causal = (col <= row) if direction == 0 else (col >= row)
    end = q - 1 if direction == 0 else 0
    first_head = lax.broadcasted_iota(jnp.int32, (1, LANES), 1) < SSM_HEAD_DIM
    keep_bf = [jnp.where(first_head, 1.0, 0.0).astype(BF16),
               jnp.where(first_head, 0.0, 1.0).astype(BF16)]

    def group_prologue(grp, ck):
        gcols = slice(grp * D_STATE, (grp + 1) * D_STATE)
        rows = slice(ck * q, (ck + 1) * q)
        bg = b_ref[0, rows, gcols]
        cg = c_ref[0, rows, gcols]
        cb = lax.dot_general(cg, bg, (((1,), (1,)), ((), ())), preferred_element_type=F32)
        bg_t = bg.astype(F32).T
        first_pair = grp * PAIRS_PER_GROUP
        s_grp = jnp.concatenate([s_ref[first_pair + k].astype(BF16) for k in range(PAIRS_PER_GROUP)], axis=1)
        y_off = jnp.dot(cg, s_grp, preferred_element_type=F32)
        return cb, bg_t, y_off

    def pair_work(grp, ck, k, cb, bg_t, y_off):
        r0 = ck * q
        rows = slice(r0, r0 + q)
        pr = grp * PAIRS_PER_GROUP + k
        x_ref = x0_ref if pr < pairs_per_tile else x1_ref
        xp = x_ref[0, rows, (pr % pairs_per_tile) * LANES:(pr % pairs_per_tile + 1) * LANES]
        ms, bws, xs, ecols, cds = [], [], [], [], []
        for i in range(HEADS_PER_PAIR):
            hc = SSM_HEADS * direction + HEADS_PER_PAIR * pr + i
            colb = jnp.broadcast_to(cum_ref[0, rows, hc:hc + 1], (q, LANES))
            m = jnp.where(causal, jnp.exp2(colb - rowb_ref[0, r0 + hc:r0 + hc + 1, :]), 0.0) * cb
            ms.append(m.astype(BF16))
            xs.append(xp * keep_bf[i])
            bws.append((bg_t * w_ref[0, r0 + hc:r0 + hc + 1, :]).astype(BF16))
            ecols.append(jnp.exp2(colb))
            cds.append(ecols[i][end:end + 1, :])
        x_split = jnp.concatenate(xs, axis=0)
        y_diag = jnp.dot(jnp.concatenate(ms, axis=1), x_split, preferred_element_type=F32)
        y_pair = y_diag + y_off[:, k * LANES:(k + 1) * LANES] * jnp.where(first_head, ecols[0], ecols[1])
        s_new = jnp.dot(jnp.concatenate(bws, axis=1), x_split, preferred_element_type=F32)
        s_ref[pr] = s_ref[pr] * jnp.where(first_head, cds[0], cds[1]) + s_new
        lanes = slice(pr * LANES, (pr + 1) * LANES)
        if final:
            y_pair = y_pair + yf_ref[0, rows, lanes].astype(F32) + xp.astype(F32) * dskip_ref[:, lanes]
        o_ref[0, rows, lanes] = y_pair.astype(o_ref.dtype)

    order = range(cps) if direction == 0 else range(cps - 1, -1, -1)
    for g0 in range(0, SSM_GROUPS, 2):
        for ck in order:
            pro = [group_prologue(g0 + d, ck) for d in range(2)]
            for k in range(PAIRS_PER_GROUP):
                for d in range(2):
                    pair_work(g0 + d, ck, k, *pro[d])


def _ssd_pass(p_main, prep, direction, final_inputs=None, *, cps=4):
    b, l, _ = p_main.shape
    rows = cps * CHUNK
    nc = l // rows
    assert l % rows == 0 and D_INNER == 2 * P_TILE and SSM_GROUPS * D_STATE == P_TILE
    chunk = (lambda c: c) if direction == 0 else (lambda c: nc - 1 - c)
    tile = lambda k: pl.BlockSpec((1, rows, P_TILE), lambda n, c: (n, chunk(c), k))
    small = pl.BlockSpec((1, rows, DT_PAD), lambda n, c: (n, chunk(c), 0))
    in_specs = [tile(P_CONV[0]), tile(P_CONV[1]), tile(P_CONV[2]), tile(P_CONV[3]), small, small, small]
    args = [p_main, p_main, p_main, p_main, *prep]
    scratch = [pltpu.VMEM((N_PAIRS, D_STATE, LANES), F32)]
    final = final_inputs is not None
    if final:
        y_fwd, dskip = final_inputs
        in_specs += [pl.BlockSpec((1, rows, D_INNER), lambda n, c: (n, chunk(c), 0)),
                     pl.BlockSpec((1, D_INNER), lambda n, c: (0, 0))]
        args += [y_fwd, dskip]
    return pl.pallas_call(
        functools.partial(_ssd_body, direction=direction, final=final, cps=cps),
        grid=(b, nc),
        in_specs=in_specs,
        out_specs=pl.BlockSpec((1, rows, D_INNER), lambda n, c: (n, chunk(c), 0)),
        out_shape=jax.ShapeDtypeStruct((b, l, D_INNER), BF16),
        scratch_shapes=scratch,
        compiler_params=_params(("parallel", "arbitrary")),
        name="ssd_bwd" if final else "ssd_fwd",
    )(*args)


def _merge_body(x_ref, yf_ref, y_ref, z0_ref, z1_ref, gf_ref, gm_ref, ng_ref, wf_ref, wm_ref, wo_ref,
                post_ref, o_ref, *, tk):
    f = jnp.dot(yf_ref[...].astype(BF16), wf_ref[...], preferred_element_type=F32)
    acc, ssq = None, None
    for c in range(D_INNER // tk):
        cols = slice(c * tk, (c + 1) * tk)
        z_ref = z0_ref if c * tk < P_TILE else z1_ref
        z = z_ref[:, (c * tk) % P_TILE:(c * tk) % P_TILE + tk].astype(F32)
        gated = y_ref[:, cols].astype(F32) * (z * jax.nn.sigmoid(z))
        sq = jnp.sum(gated * gated, axis=-1, keepdims=True)
        part = jnp.dot((gated * ng_ref[:, cols]).astype(BF16), wm_ref[cols, :], preferred_element_type=F32)
        acc = part if acc is None else acc + part
        ssq = sq if ssq is None else ssq + sq
    m = acc * lax.rsqrt(ssq * (1.0 / D_INNER) + EPS)
    merged = (jax.nn.sigmoid(gf_ref[...].astype(F32)) * f
              + jax.nn.sigmoid(gm_ref[...].astype(F32)) * m).astype(BF16)
    out = jnp.dot(merged, wo_ref[...], preferred_element_type=F32)
    o_ref[...] = x_ref[...] + _rms(out, post_ref[...])


def _merge_block(x, yf, y_ssd, p_main, norm_g, w_f, w_m, w_o, post_g, *, tm=512, tk=256):
    t = x.shape[0]
    assert t % tm == 0 and P_TILE == D_MODEL and P_TILE % tk == 0
    tile = lambda k: pl.BlockSpec((tm, P_TILE), lambda i: (i, k))
    return pl.pallas_call(
        functools.partial(_merge_body, tk=tk),
        grid=(t // tm,),
        in_specs=[pl.BlockSpec((tm, D_MODEL), lambda i: (i, 0)),
                  pl.BlockSpec((tm, D_FOURIER), lambda i: (i, 0)),
                  pl.BlockSpec((tm, D_INNER), lambda i: (i, 0)),
                  tile(P_PLAIN[0]), tile(P_PLAIN[1]), tile(P_PLAIN[2]), tile(P_PLAIN[3]),
                  _resident((1, D_INNER)),
                  _resident((D_FOURIER, D_MODEL)), _resident((D_INNER, D_MODEL)),
                  _resident((D_MODEL, D_MODEL)), _resident((1, D_MODEL))],
        out_specs=pl.BlockSpec((tm, D_MODEL), lambda i: (i, 0)),
        out_shape=jax.ShapeDtypeStruct((t, D_MODEL), F32),
        compiler_params=_params(("parallel",)),
        name="merge_out",
    )(x, yf, y_ssd, p_main, p_main, p_main, p_main, norm_g, w_f, w_m, w_o, post_g)


def _prepare(ffn1_pre_g, ffn1_post_g, ffn1_w_in, ffn1_w_out, mix_pre_g, mix_post_g, w_in, conv_w,
             conv_b, dt_bias, a_log, d_skip, ssm_norm_g, w_branch_f, w_branch_m, w_out,
             ffn2_pre_g, ffn2_post_g, ffn2_w_in, ffn2_w_out):
    row = lambda v: v.reshape(1, -1).astype(F32)
    off_f, off_z = D_FOURIER, D_FOURIER + D_INNER
    off_xbc = off_z + CONV_DIM
    off_dt = off_xbc + 2 * SSM_HEADS
    plain = jnp.concatenate([w_in[:, off_f:off_z], w_in[:, off_dt:]], axis=1)
    conv = w_in[:, off_z:off_xbc]
    w_main = jnp.concatenate([part[:, k * P_TILE:(k + 1) * P_TILE]
                              for k in range(len(P_PLAIN)) for part in (plain, conv)], axis=1).astype(BF16)
    pad = DT_PAD - 2 * SSM_HEADS
    w_dt = jnp.pad(w_in[:, off_xbc:off_dt], ((0, 0), (0, pad))).astype(BF16)
    return dict(
        ffn1=(row(ffn1_pre_g), row(ffn1_post_g), ffn1_w_in.astype(BF16), ffn1_w_out.astype(BF16)),
        ffn2=(row(ffn2_pre_g), row(ffn2_post_g), ffn2_w_in.astype(BF16), ffn2_w_out.astype(BF16)),
        mix_pre=row(mix_pre_g), mix_post=row(mix_post_g),
        w_main=w_main, w_four=w_in[:, :off_f].astype(BF16), w_dt=w_dt,
        conv_w=conv_w.astype(F32), conv_b=row(conv_b),
        dt_bias=jnp.pad(row(dt_bias), ((0, 0), (0, pad))),
        a_log=jnp.pad(row(a_log), ((0, 0), (0, pad))),
        d_skip=row(jnp.repeat(d_skip, SSM_HEAD_DIM)), norm_g=row(ssm_norm_g),
        w_f=w_branch_f.astype(BF16), w_m=w_branch_m.astype(BF16), w_o=w_out.astype(BF16))


def _encoder_layer(x, p):
    b, l, d = x.shape
    t = b * l
    x = _ffn_block(x.reshape(t, d), *p["ffn1"])
    p_main = _proj_main(x, p["mix_pre"], p["w_main"], p["conv_w"], p["conv_b"], l).reshape(b, l, P_MAIN)
    u_f, dt_raw = _fourier_proj(x, p["mix_pre"], p["w_four"], p["w_dt"], b, l)
    dt_raw = dt_raw.reshape(b, l, DT_PAD)
    yf = _fourier_branch(u_f, l)
    prep = _ssd_prep(dt_raw, p["dt_bias"], p["a_log"])
    y_fwd = _ssd_pass(p_main, prep, 0)
    y_ssd = _ssd_pass(p_main, prep, 1, final_inputs=(y_fwd, p["d_skip"]))
    x = _merge_block(x, yf.reshape(t, D_FOURIER), y_ssd.reshape(t, D_INNER), p_main.reshape(t, P_MAIN),
                     p["norm_g"], p["w_f"], p["w_m"], p["w_o"], p["mix_post"])
    x = _ffn_block(x, *p["ffn2"])
    return x.reshape(b, l, d)


def kernel(x_prompt, x_sample, ffn1_pre_g, ffn1_post_g, ffn1_w_in, ffn1_w_out, mix_pre_g, mix_post_g, w_in, conv_w, conv_b, dt_bias, a_log, d_skip, ssm_norm_g, w_branch_f, w_branch_m, w_out, ffn2_pre_g, ffn2_post_g, ffn2_w_in, ffn2_w_out):
    layers = (ffn1_pre_g, ffn1_post_g, ffn1_w_in, ffn1_w_out, mix_pre_g, mix_post_g, w_in, conv_w,
              conv_b, dt_bias, a_log, d_skip, ssm_norm_g, w_branch_f, w_branch_m, w_out,
              ffn2_pre_g, ffn2_post_g, ffn2_w_in, ffn2_w_out)
    y_prompt, y_sample = x_prompt, x_sample
    for layer in range(ffn1_pre_g.shape[0]):
        p = _prepare(*(a[layer] for a in layers))
        y_prompt = _encoder_layer(y_prompt, p)
        y_sample = _encoder_layer(y_sample, p)
    return (y_prompt, y_sample)
```

```python
import functools
import math

import numpy as np
import jax
import jax.numpy as jnp
from jax import lax
from jax.experimental import pallas as pl
from jax.experimental.pallas import tpu as pltpu

F32 = jnp.float32
BF16 = jnp.bfloat16

D_MODEL = 1024
D_FF = 2816
FOURIER_GROUP_WIDTH = 256
D_FOURIER = 1024
SSM_HEADS = 32
SSM_HEAD_DIM = 64
D_INNER = SSM_HEADS * SSM_HEAD_DIM
SSM_GROUPS = 8
D_STATE = 128
CONV_WIDTH = 5
CONV_PAD = CONV_WIDTH // 2
CHUNK = 128
CONV_DIM = D_INNER + 2 * SSM_GROUPS * D_STATE
EPS = 1e-6

LANES = 128
MXU_DIM = 256
BF16_SUBLANES = 16
VMEM_LIMIT = 56 * 1024 * 1024

P_TILE = 1024
P_PLAIN = (0, 2, 4, 6)
P_CONV = (1, 3, 5, 7)
P_MAIN = (len(P_PLAIN) + len(P_CONV)) * P_TILE
W_IN_CONV_TILE0 = (D_FOURIER + D_INNER) // P_TILE
DT_PAD = LANES


def _params(sem):
    return pltpu.CompilerParams(dimension_semantics=sem, vmem_limit_bytes=VMEM_LIMIT)


def _rms(x, g):
    return x * lax.rsqrt(jnp.mean(x * x, axis=-1, keepdims=True) + EPS) * g


def _resident(shape):
    nd = len(shape)
    return pl.BlockSpec(shape, lambda *_: (0,) * nd, pipeline_mode=pl.Buffered(1))


def _ffn_body(x_ref, pre_ref, post_ref, win_ref, wout_ref, o_ref, h_ref, *, tf):
    x = x_ref[...]
    h_ref[...] = _rms(x, pre_ref[...]).astype(BF16)
    acc = None
    for c in range(D_FF // tf):
        h = h_ref[...]
        gate = jnp.dot(h, win_ref[:, c * tf:(c + 1) * tf], preferred_element_type=F32)
        up = jnp.dot(h, win_ref[:, D_FF + c * tf:D_FF + (c + 1) * tf], preferred_element_type=F32)
        act = (gate * jax.nn.sigmoid(gate) * up).astype(BF16)
        part = jnp.dot(act, wout_ref[c * tf:(c + 1) * tf, :], preferred_element_type=F32)
        acc = part if acc is None else acc + part
    o_ref[...] = x + 0.5 * _rms(acc, post_ref[...])


def _ffn_block(x, pre_g, post_g, w_in, w_out, *, tm=512, tf=256):
    t = x.shape[0]
    assert t % tm == 0 and D_FF % tf == 0
    return pl.pallas_call(
        functools.partial(_ffn_body, tf=tf),
        grid=(t // tm,),
        in_specs=[pl.BlockSpec((tm, D_MODEL), lambda i: (i, 0)),
                  _resident((1, D_MODEL)), _resident((1, D_MODEL)),
                  _resident((D_MODEL, 2 * D_FF)), _resident((D_FF, D_MODEL))],
        out_specs=pl.BlockSpec((tm, D_MODEL), lambda i: (i, 0)),
        out_shape=jax.ShapeDtypeStruct((t, D_MODEL), F32),
        scratch_shapes=[pltpu.VMEM((tm, D_MODEL), BF16)],
        compiler_params=_params(("parallel",)),
        name="ffn_block",
    )(x, pre_g, post_g, w_in, w_out)


FOURIER_L1 = MXU_DIM
SUBLANES = 8


def _fourier_tables(l):
    l1 = FOURIER_L1
    l2 = l // l1
    g = MXU_DIM // l2
    w = FOURIER_GROUP_WIDTH
    c = np.arange(w)
    ang = 2.0 * np.pi * np.outer(c, c) / w
    cs_w = np.concatenate([np.cos(ang), -np.sin(ang)], axis=0) / math.sqrt(w)
    n1 = np.arange(l1)
    ang1 = 2.0 * np.pi * np.outer(n1, n1) / l1
    cs1 = np.concatenate([np.cos(ang1), np.sin(ang1)], axis=0)
    k1 = np.arange(l1).reshape(l1 // g, 1, g, 1, 1)
    k2 = np.arange(l2).reshape(1, l2, 1, 1, 1)
    n2 = np.arange(l2).reshape(1, 1, 1, 1, l2)
    theta = 2.0 * np.pi * (n2 * k2 / l2 + n2 * k1 / l)
    eye = np.eye(g).reshape(1, 1, g, g, 1)

    def tiled_cols(m):
        m = m.reshape(l1 // g, l2 * g, g, l2 // SUBLANES, SUBLANES)
        return m.transpose(0, 1, 3, 2, 4).reshape(l1 // g, l2 * g, g * l2)

    gc = tiled_cols(np.cos(theta) * eye)
    gs = tiled_cols(np.sin(theta) * eye)
    g2 = np.concatenate([np.concatenate([gc, -gs], axis=2),
                         np.concatenate([gs, gc], axis=2)], axis=1) / math.sqrt(l)
    as_bf16 = lambda a: jnp.asarray(a, dtype=F32).astype(BF16)
    return as_bf16(cs_w), as_bf16(cs1), as_bf16(g2)


def _fourier_proj_body(x_ref, g_ref, w_ref, wdt_ref, o_ref, dt_ref, *, l2):
    h = _rms(x_ref[...], g_ref[...]).astype(BF16)
    dt_ref[...] = jnp.dot(h, wdt_ref[...], preferred_element_type=F32)
    u = jnp.dot(h, w_ref[...], preferred_element_type=F32)
    tm, tn = u.shape
    u = u.reshape(tm // l2, l2 // SUBLANES, SUBLANES, tn)
    for t in range(l2 // SUBLANES):
        ut = u[:, t].reshape(tm // l2 * SUBLANES, tn)
        for s in range(tn // LANES):
            o_ref[0, t, s] = ut[:, s * LANES:(s + 1) * LANES]


def _fourier_proj(x, g, w_all, w_dt, b, l, *, tm=1024):
    t, c = x.shape[0], D_FOURIER
    l1 = FOURIER_L1
    l2 = l // l1
    per_seq = l // tm
    assert t == b * l and l % tm == 0 and tm % l2 == 0 and l2 % SUBLANES == 0 and c % LANES == 0
    return pl.pallas_call(
        functools.partial(_fourier_proj_body, l2=l2),
        grid=(t // tm,),
        in_specs=[pl.BlockSpec((tm, D_MODEL), lambda i: (i, 0)),
                  _resident((1, D_MODEL)), _resident((D_MODEL, c)), _resident((D_MODEL, DT_PAD))],
        out_specs=[pl.BlockSpec((1, l2 // SUBLANES, c // LANES, tm // l2 * SUBLANES, LANES),
                                lambda i: (i // per_seq, 0, 0, i % per_seq, 0)),
                   pl.BlockSpec((tm, DT_PAD), lambda i: (i, 0))],
        out_shape=[jax.ShapeDtypeStruct((b, l2 // SUBLANES, c // LANES, l1 * SUBLANES, LANES), F32),
                   jax.ShapeDtypeStruct((t, DT_PAD), F32)],
        compiler_params=_params(("parallel",)),
        name="fourier_proj",
    )(x, g, w_all, w_dt)


def _pack_bf16_pair(hi, lo):
    bits = lambda v: lax.bitcast_convert_type(v.astype(BF16).astype(F32), jnp.uint32)
    return bits(hi) | (bits(lo) >> 16)


def _unpack_bf16_pair(word):
    hi = lax.bitcast_convert_type(word & jnp.uint32(0xFFFF0000), F32).astype(BF16)
    lo = lax.bitcast_convert_type(word << 16, F32).astype(BF16)
    return hi, lo


def _fourier1_body(u_ref, cs1_ref, z_ref):
    l1 = FOURIER_L1
    slabs = u_ref.shape[0]
    for j in range(SUBLANES):
        pick = pl.ds(j, l1, stride=SUBLANES)
        u = jnp.concatenate([u_ref.at[s][pick, :] for s in range(slabs)], axis=1).astype(BF16)
        z = jnp.dot(cs1_ref[...], u, preferred_element_type=F32)
        word = _pack_bf16_pair(z[:l1], z[l1:])
        for s in range(slabs):
            z_ref.at[s][pick, :] = word[:, s * LANES:(s + 1) * LANES]


def _fourier2_body(z_ref, g2_ref, csw_ref, o_ref, *, tiles, g):
    w = FOURIER_GROUP_WIDTH
    half = MXU_DIM
    slabs = z_ref.shape[2]
    rows = g * SUBLANES
    for t in range(tiles):
        parts = [_unpack_bf16_pair(z_ref[0, :, s, t * rows:(t + 1) * rows, :].reshape(half, LANES))
                 for s in range(slabs)]
        z = jnp.concatenate([jnp.concatenate([p[0] for p in parts], axis=1),
                             jnp.concatenate([p[1] for p in parts], axis=1)], axis=0)
        v = jnp.dot(g2_ref[t], z, preferred_element_type=F32).astype(BF16)
        for s in range(z.shape[1] // w):
            cols = slice(s * w, (s + 1) * w)
            vv = jnp.concatenate([v[:half, cols], v[half:, cols]], axis=1)
            res = jnp.dot(vv, csw_ref[...], preferred_element_type=F32)
            o_ref[0, :, t * g:(t + 1) * g, cols] = res.reshape(o_ref.shape[1], g, w)


def _fourier_branch(u, l, *, slabs_per_step=8, tiles_per_step=4):
    b, nt, ns, rows, _ = u.shape
    c = ns * LANES
    l1 = FOURIER_L1
    l2 = l // l1
    g = MXU_DIM // l2
    assert nt * SUBLANES == l2 and rows == l1 * SUBLANES and MXU_DIM % l2 == 0 and g % SUBLANES == 0
    assert ns % slabs_per_step == 0
    csw, cs1, g2 = _fourier_tables(l)
    blk = pl.BlockSpec((None, None, slabs_per_step, rows, LANES), lambda i, j, k: (i, j, k, 0, 0))
    z = pl.pallas_call(
        _fourier1_body,
        grid=(b, nt, ns // slabs_per_step),
        in_specs=[blk, _resident(cs1.shape)],
        out_specs=blk,
        out_shape=jax.ShapeDtypeStruct(u.shape, jnp.uint32),
        compiler_params=_params(("parallel", "parallel", "parallel")),
        name="fourier_stage1",
    )(u, cs1)
    tiles = tiles_per_step
    assert (l1 // g) % tiles == 0
    y = pl.pallas_call(
        functools.partial(_fourier2_body, tiles=tiles, g=g),
        grid=(b, l1 // g // tiles),
        in_specs=[pl.BlockSpec((1, nt, ns, tiles * g * SUBLANES, LANES), lambda i, t: (i, 0, 0, t, 0)),
                  pl.BlockSpec((tiles, 2 * MXU_DIM, 2 * MXU_DIM), lambda i, t: (t, 0, 0)),
                  _resident(csw.shape)],
        out_specs=pl.BlockSpec((1, l2, tiles * g, c), lambda i, t: (i, 0, t, 0)),
        out_shape=jax.ShapeDtypeStruct((b, l2, l1, c), F32),
        compiler_params=_params(("parallel", "parallel")),
        name="fourier_stage2",
    )(z, g2, csw)
    return y.reshape(b, l, c)


CONV_HALO = BF16_SUBLANES
CONV_PHASES = 4


def _proj_main_body(xp_ref, x_ref, xn_ref, g_ref, wp_ref, wc_ref, cw_ref, cb_ref, o_ref, h_ref, ext_ref, act_ref,
                    *, tm, tiles_per_seq):
    i, j = pl.program_id(0), pl.program_id(1)
    halo = CONV_HALO

    @pl.when(j == 0)
    def _():
        h_ref[0:halo, :] = _rms(xp_ref[...], g_ref[...]).astype(BF16)
        h_ref[halo:halo + tm, :] = _rms(x_ref[...], g_ref[...]).astype(BF16)
        h_ref[halo + tm:, :] = _rms(xn_ref[...], g_ref[...]).astype(BF16)

    seq_first = (i % tiles_per_seq) == 0
    seq_last = (i % tiles_per_seq) == tiles_per_seq - 1
    prod = jnp.dot(h_ref[...], wc_ref[...], preferred_element_type=F32)
    n_slabs = P_TILE // LANES
    for s in range(n_slabs):
        ext_ref[s] = prod[:, s * LANES:(s + 1) * LANES]
    ext_ref[:, 0:halo, :] = jnp.where(seq_first, 0.0, ext_ref[:, 0:halo, :])
    ext_ref[:, halo + tm:, :] = jnp.where(seq_last, 0.0, ext_ref[:, halo + tm:, :])

    o_ref[:, :P_TILE] = jnp.dot(h_ref[halo:halo + tm, :], wp_ref[...],
                                preferred_element_type=F32).astype(o_ref.dtype)
    n = tm // CONV_PHASES
    for s in range(n_slabs):
        lanes = slice(s * LANES, (s + 1) * LANES)
        ext, act = ext_ref.at[s], act_ref.at[s]
        taps = {m: ext[pl.ds(halo + m, n, stride=CONV_PHASES), :]
                for m in range(-CONV_PAD, CONV_PHASES + CONV_PAD)}
        for j in range(CONV_PHASES):
            acc = cb_ref[:, lanes]
            for k in range(CONV_WIDTH):
                acc = acc + cw_ref[k:k + 1, lanes] * taps[j + k - CONV_PAD]
            act[pl.ds(j, n, stride=CONV_PHASES), :] = acc * jax.nn.sigmoid(acc)
    for s in range(n_slabs):
        o_ref[:, P_TILE + s * LANES:P_TILE + (s + 1) * LANES] = act_ref[s].astype(o_ref.dtype)


def _proj_main(x, g, w_plain, w_all, conv_tile0, conv_w, conv_b, l, *, tm=1024):
    t, n = x.shape[0], 2 * w_plain.shape[1]
    halo = CONV_HALO
    pair = 2 * P_TILE
    assert t % tm == 0 and l % tm == 0 and n == P_MAIN and tm % halo == 0
    assert P_PLAIN == tuple(range(0, 2 * len(P_PLAIN), 2)) and P_CONV == tuple(range(1, 2 * len(P_CONV), 2))
    assert (conv_tile0 + len(P_CONV)) * P_TILE <= w_all.shape[1]
    hb = tm // halo
    nhb = t // halo
    return pl.pallas_call(
        functools.partial(_proj_main_body, tm=tm, tiles_per_seq=l // tm),
        grid=(t // tm, n // pair),
        in_specs=[pl.BlockSpec((halo, D_MODEL), lambda i, j: (jnp.maximum(i * hb - 1, 0), 0)),
                  pl.BlockSpec((tm, D_MODEL), lambda i, j: (i, 0)),
                  pl.BlockSpec((halo, D_MODEL), lambda i, j: (jnp.minimum((i + 1) * hb, nhb - 1), 0)),
                  pl.BlockSpec((1, D_MODEL), lambda i, j: (0, 0)),
                  pl.BlockSpec((D_MODEL, P_TILE), lambda i, j: (0, j)),
                  pl.BlockSpec((D_MODEL, P_TILE), lambda i, j: (0, conv_tile0 + j)),
                  pl.BlockSpec((CONV_WIDTH, P_TILE), lambda i, j: (0, j)),
                  pl.BlockSpec((1, P_TILE), lambda i, j: (0, j))],
        out_specs=pl.BlockSpec((tm, pair), lambda i, j: (i, j)),
        out_shape=jax.ShapeDtypeStruct((t, n), BF16),
        scratch_shapes=[pltpu.VMEM((tm + 2 * halo, D_MODEL), BF16),
                        pltpu.VMEM((P_TILE // LANES, tm + 2 * halo, LANES), F32),
                        pltpu.VMEM((P_TILE // LANES, tm, LANES), F32)],
        compiler_params=_params(("parallel", "arbitrary")),
        name="proj_main",
    )(x, x, x, g, w_plain, w_all, conv_w, conv_b)


LOG2E = math.log2(math.e)

HEADS_PER_PAIR = LANES // SSM_HEAD_DIM
N_PAIRS = SSM_HEADS // HEADS_PER_PAIR
HEADS_PER_GROUP = SSM_HEADS // SSM_GROUPS
PAIRS_PER_GROUP = HEADS_PER_GROUP // HEADS_PER_PAIR


def _split3(a):
    hi = a.astype(BF16)
    r1 = a - hi.astype(F32)
    mid = r1.astype(BF16)
    lo = (r1 - mid.astype(F32)).astype(BF16)
    return hi, mid, lo


def _ssd_prep_body(dt_ref, bias_ref, alog_ref, cum_ref, rowb_ref, w_ref, *, chunks):
    q = CHUNK
    row = lax.broadcasted_iota(jnp.int32, (q, q), 0)
    col = lax.broadcasted_iota(jnp.int32, (q, q), 1)
    tri = jnp.concatenate([jnp.where(col <= row, 1.0, 0.0), jnp.where(col >= row, 1.0, 0.0)],
                          axis=0).astype(BF16)
    fwd_lane = lax.broadcasted_iota(jnp.int32, (q, LANES), 1) < SSM_HEADS
    neg_a = -jnp.exp(alog_ref[...]) * LOG2E
    for c in range(chunks):
        rows = slice(c * q, (c + 1) * q)
        dtv = jax.nn.softplus(dt_ref[0, rows, :] + bias_ref[...])
        a = dtv * neg_a
        both = sum(jnp.dot(tri, piece, preferred_element_type=F32) for piece in _split3(a))
        cum2 = jnp.where(fwd_lane, both[:q], both[q:])
        cend = jnp.where(fwd_lane[0:1], cum2[q - 1:q], cum2[0:1])
        cum_ref[0, rows, :] = cum2
        rowb_ref[0, rows, :] = (cum2 - jnp.log2(dtv)).T
        w_ref[0, rows, :] = (jnp.exp2(cend - cum2) * dtv).T


def _ssd_prep(dt_raw, bias, alog, *, chunks=4):
    b, l, _ = dt_raw.shape
    rows = chunks * CHUNK
    assert l % rows == 0
    spec = pl.BlockSpec((1, rows, DT_PAD), lambda n, c: (n, c, 0))
    vec = pl.BlockSpec((1, DT_PAD), lambda n, c: (0, 0))
    shape = jax.ShapeDtypeStruct((b, l, DT_PAD), F32)
    return pl.pallas_call(
        functools.partial(_ssd_prep_body, chunks=chunks),
        grid=(b, l // rows),
        in_specs=[spec, vec, vec],
        out_specs=[spec, spec, spec],
        out_shape=[shape, shape, shape],
        compiler_params=_params(("parallel", "parallel")),
        name="ssd_prep",
    )(dt_raw, bias, alog)


def _ssd_body(*refs, direction, final, cps):
    if final:
        x0_ref, x1_ref, b_ref, c_ref, cum_ref, rowb_ref, w_ref, yf_ref, dskip_ref, o_ref, s_ref = refs
    else:
        x0_ref, x1_ref, b_ref, c_ref, cum_ref, rowb_ref, w_ref, o_ref, s_ref = refs
    q = CHUNK
    pairs_per_tile = P_TILE // LANES

    @pl.when(pl.program_id(1) == 0)
    def _():
        s_ref[...] = jnp.zeros_like(s_ref)

    row = lax.broadcasted_iota(jnp.int32, (q, q), 0)
    col = lax.broadcasted_iota(jnp.int32, (q, q), 1)
    causal = (col <= row) if direction == 0 else (col >= row)
    end = q - 1 if direction == 0 else 0
    first_head = lax.broadcasted_iota(jnp.int32, (1, LANES), 1) < SSM_HEAD_DIM
    keep_bf = [jnp.where(first_head, 1.0, 0.0).astype(BF16),
               jnp.where(first_head, 0.0, 1.0).astype(BF16)]

    def group_prologue(grp, ck):
        gcols = slice(grp * D_STATE, (grp + 1) * D_STATE)
        rows = slice(ck * q, (ck + 1) * q)
        bg = b_ref[0, rows, gcols]
        cg = c_ref[0, rows, gcols]
        cb = lax.dot_general(cg, bg, (((1,), (1,)), ((), ())), preferred_element_type=F32)
        bg_t = bg.astype(F32).T
        first_pair = grp * PAIRS_PER_GROUP
        s_grp = jnp.concatenate([s_ref[first_pair + k].astype(BF16) for k in range(PAIRS_PER_GROUP)], axis=1)
        y_off = jnp.dot(cg, s_grp, preferred_element_type=F32)
        return cb, bg_t, y_off

    def pair_work(grp, ck, k, cb, bg_t, y_off):
        r0 = ck * q
        rows = slice(r0, r0 + q)
        pr = grp * PAIRS_PER_GROUP + k
        x_ref = x0_ref if pr < pairs_per_tile else x1_ref
        xp = x_ref[0, rows, (pr % pairs_per_tile) * LANES:(pr % pairs_per_tile + 1) * LANES]
        ms, bws, xs, cums = [], [], [], []
        for i in range(HEADS_PER_PAIR):
            hc = SSM_HEADS * direction + HEADS_PER_PAIR * pr + i
            colb = jnp.broadcast_to(cum_ref[0, rows, hc:hc + 1], (q, LANES))
            m = jnp.where(causal, jnp.exp2(colb - rowb_ref[0, r0 + hc:r0 + hc + 1, :]), 0.0) * cb
            ms.append(m.astype(BF16))
            xs.append(xp * keep_bf[i])
            bws.append((bg_t * w_ref[0, r0 + hc:r0 + hc + 1, :]).astype(BF16))
            cums.append(colb)
        x_split = jnp.concatenate(xs, axis=0)
        y_diag = jnp.dot(jnp.concatenate(ms, axis=1), x_split, preferred_element_type=F32)
        ecol = jnp.exp2(jnp.where(first_head, cums[0], cums[1]))
        y_pair = y_diag + y_off[:, k * LANES:(k + 1) * LANES] * ecol
        s_new = jnp.dot(jnp.concatenate(bws, axis=1), x_split, preferred_element_type=F32)
        s_ref[pr] = s_ref[pr] * ecol[end:end + 1, :] + s_new
        lanes = slice(pr * LANES, (pr + 1) * LANES)
        if final:
            y_pair = y_pair + yf_ref[0, rows, lanes].astype(F32) + xp.astype(F32) * dskip_ref[:, lanes]
        o_ref[0, rows, lanes] = y_pair.astype(o_ref.dtype)

    order = range(cps) if direction == 0 else range(cps - 1, -1, -1)
    for g0 in range(0, SSM_GROUPS, 2):
        for ck in order:
            pro = [group_prologue(g0 + d, ck) for d in range(2)]
            for k in range(PAIRS_PER_GROUP):
                for d in range(2):
                    pair_work(g0 + d, ck, k, *pro[d])


def _ssd_pass(p_main, prep, direction, final_inputs=None, *, cps=8):
    b, l, _ = p_main.shape
    rows = cps * CHUNK
    nc = l // rows
    assert l % rows == 0 and D_INNER == 2 * P_TILE and SSM_GROUPS * D_STATE == P_TILE
    chunk = (lambda c: c) if direction == 0 else (lambda c: nc - 1 - c)
    tile = lambda k: pl.BlockSpec((1, rows, P_TILE), lambda n, c: (n, chunk(c), k))
    small = pl.BlockSpec((1, rows, DT_PAD), lambda n, c: (n, chunk(c), 0))
    in_specs = [tile(P_CONV[0]), tile(P_CONV[1]), tile(P_CONV[2]), tile(P_CONV[3]), small, small, small]
    args = [p_main, p_main, p_main, p_main, *prep]
    scratch = [pltpu.VMEM((N_PAIRS, D_STATE, LANES), F32)]
    final = final_inputs is not None
    if final:
        y_fwd, dskip = final_inputs
        in_specs += [pl.BlockSpec((1, rows, D_INNER), lambda n, c: (n, chunk(c), 0)),
                     pl.BlockSpec((1, D_INNER), lambda n, c: (0, 0))]
        args += [y_fwd, dskip]
    return pl.pallas_call(
        functools.partial(_ssd_body, direction=direction, final=final, cps=cps),
        grid=(b, nc),
        in_specs=in_specs,
        out_specs=pl.BlockSpec((1, rows, D_INNER), lambda n, c: (n, chunk(c), 0)),
        out_shape=jax.ShapeDtypeStruct((b, l, D_INNER), BF16),
        scratch_shapes=scratch,
        compiler_params=_params(("parallel", "arbitrary")),
        name="ssd_bwd" if final else "ssd_fwd",
    )(*args)


def _merge_body(x_ref, yf_ref, y_ref, z0_ref, z1_ref, gf_ref, gm_ref, ng_ref, wf_ref, wm_ref, wo_ref,
                post_ref, o_ref, *, tk):
    f = jnp.dot(yf_ref[...].astype(BF16), wf_ref[...], preferred_element_type=F32)
    acc, ssq = None, None
    for c in range(D_INNER // tk):
        cols = slice(c * tk, (c + 1) * tk)
        z_ref = z0_ref if c * tk < P_TILE else z1_ref
        z = z_ref[:, (c * tk) % P_TILE:(c * tk) % P_TILE + tk].astype(F32)
        gated = y_ref[:, cols].astype(F32) * (z * jax.nn.sigmoid(z))
        sq = jnp.sum(gated * gated, axis=-1, keepdims=True)
        part = jnp.dot((gated * ng_ref[:, cols]).astype(BF16), wm_ref[cols, :], preferred_element_type=F32)
        acc = part if acc is None else acc + part
        ssq = sq if ssq is None else ssq + sq
    m = acc * lax.rsqrt(ssq * (1.0 / D_INNER) + EPS)
    merged = (jax.nn.sigmoid(gf_ref[...].astype(F32)) * f
              + jax.nn.sigmoid(gm_ref[...].astype(F32)) * m).astype(BF16)
    out = jnp.dot(merged, wo_ref[...], preferred_element_type=F32)
    o_ref[...] = x_ref[...] + _rms(out, post_ref[...])


def _merge_block(x, yf, y_ssd, p_main, norm_g, w_f, w_m, w_o, post_g, *, tm=512, tk=256):
    t = x.shape[0]
    assert t % tm == 0 and P_TILE == D_MODEL and P_TILE % tk == 0
    tile = lambda k: pl.BlockSpec((tm, P_TILE), lambda i: (i, k))
    return pl.pallas_call(
        functools.partial(_merge_body, tk=tk),
        grid=(t // tm,),
        in_specs=[pl.BlockSpec((tm, D_MODEL), lambda i: (i, 0)),
                  pl.BlockSpec((tm, D_FOURIER), lambda i: (i, 0)),
                  pl.BlockSpec((tm, D_INNER), lambda i: (i, 0)),
                  tile(P_PLAIN[0]), tile(P_PLAIN[1]), tile(P_PLAIN[2]), tile(P_PLAIN[3]),
                  _resident((1, D_INNER)),
                  _resident((D_FOURIER, D_MODEL)), _resident((D_INNER, D_MODEL)),
                  _resident((D_MODEL, D_MODEL)), _resident((1, D_MODEL))],
        out_specs=pl.BlockSpec((tm, D_MODEL), lambda i: (i, 0)),
        out_shape=jax.ShapeDtypeStruct((t, D_MODEL), F32),
        compiler_params=_params(("parallel",)),
        name="merge_out",
    )(x, yf, y_ssd, p_main, p_main, p_main, p_main, norm_g, w_f, w_m, w_o, post_g)


def _prepare(ffn1_pre_g, ffn1_post_g, ffn1_w_in, ffn1_w_out, mix_pre_g, mix_post_g, w_in, conv_w,
             conv_b, dt_bias, a_log, d_skip, ssm_norm_g, w_branch_f, w_branch_m, w_out,
             ffn2_pre_g, ffn2_post_g, ffn2_w_in, ffn2_w_out):
    row = lambda v: v.reshape(1, -1).astype(F32)
    off_f, off_z = D_FOURIER, D_FOURIER + D_INNER
    off_xbc = off_z + CONV_DIM
    off_dt = off_xbc + 2 * SSM_HEADS
    assert off_z == W_IN_CONV_TILE0 * P_TILE
    w_all = w_in.astype(BF16)
    w_plain = jnp.concatenate([w_all[:, off_f:off_z], w_all[:, off_dt:]], axis=1)
    pad = DT_PAD - 2 * SSM_HEADS
    w_dt = jnp.pad(w_all[:, off_xbc:off_dt], ((0, 0), (0, pad)))
    return dict(
        ffn1=(row(ffn1_pre_g), row(ffn1_post_g), ffn1_w_in.astype(BF16), ffn1_w_out.astype(BF16)),
        ffn2=(row(ffn2_pre_g), row(ffn2_post_g), ffn2_w_in.astype(BF16), ffn2_w_out.astype(BF16)),
        mix_pre=row(mix_pre_g), mix_post=row(mix_post_g),
        w_all=w_all, w_plain=w_plain, w_dt=w_dt,
        conv_w=conv_w.astype(F32), conv_b=row(conv_b),
        dt_bias=jnp.pad(row(dt_bias), ((0, 0), (0, pad))),
        a_log=jnp.pad(row(a_log), ((0, 0), (0, pad))),
        d_skip=row(jnp.repeat(d_skip, SSM_HEAD_DIM)), norm_g=row(ssm_norm_g),
        w_f=w_branch_f.astype(BF16), w_m=w_branch_m.astype(BF16), w_o=w_out.astype(BF16))


def _encoder_layer(x, p):
    b, l, d = x.shape
    t = b * l
    x = _ffn_block(x.reshape(t, d), *p["ffn1"])
    p_main = _proj_main(x, p["mix_pre"], p["w_plain"], p["w_all"], W_IN_CONV_TILE0, p["conv_w"],
                        p["conv_b"], l).reshape(b, l, P_MAIN)
    u_f, dt_raw = _fourier_proj(x, p["mix_pre"], p["w_all"], p["w_dt"], b, l)
    dt_raw = dt_raw.reshape(b, l, DT_PAD)
    yf = _fourier_branch(u_f, l)
    prep = _ssd_prep(dt_raw, p["dt_bias"], p["a_log"])
    y_fwd = _ssd_pass(p_main, prep, 0)
    y_ssd = _ssd_pass(p_main, prep, 1, final_inputs=(y_fwd, p["d_skip"]))
    x = _merge_block(x, yf.reshape(t, D_FOURIER), y_ssd.reshape(t, D_INNER), p_main.reshape(t, P_MAIN),
                     p["norm_g"], p["w_f"], p["w_m"], p["w_o"], p["mix_post"])
    x = _ffn_block(x, *p["ffn2"])
    return x.reshape(b, l, d)


def kernel(x_prompt, x_sample, ffn1_pre_g, ffn1_post_g, ffn1_w_in, ffn1_w_out, mix_pre_g, mix_post_g, w_in, conv_w, conv_b, dt_bias, a_log, d_skip, ssm_norm_g, w_branch_f, w_branch_m, w_out, ffn2_pre_g, ffn2_post_g, ffn2_w_in, ffn2_w_out):
    layers = (ffn1_pre_g, ffn1_post_g, ffn1_w_in, ffn1_w_out, mix_pre_g, mix_post_g, w_in, conv_w,
              conv_b, dt_bias, a_log, d_skip, ssm_norm_g, w_branch_f, w_branch_m, w_out,
              ffn2_pre_g, ffn2_post_g, ffn2_w_in, ffn2_w_out)
    y_prompt, y_sample = x_prompt, x_sample
    for layer in range(ffn1_pre_g.shape[0]):
        p = _prepare(*(a[layer] for a in layers))
        y_prompt = _encoder_layer(y_prompt, p)
        y_sample = _encoder_layer(y_sample, p)
    return (y_prompt, y_sample)
```

```python
import functools
import math

import numpy as np
import jax
import jax.numpy as jnp
from jax import lax
from jax.experimental import pallas as pl
from jax.experimental.pallas import tpu as pltpu

F32 = jnp.float32
BF16 = jnp.bfloat16

D_MODEL = 1024
D_FF = 2816
FOURIER_GROUP_WIDTH = 256
D_FOURIER = 1024
SSM_HEADS = 32
SSM_HEAD_DIM = 64
D_INNER = SSM_HEADS * SSM_HEAD_DIM
SSM_GROUPS = 8
D_STATE = 128
CONV_WIDTH = 5
CONV_PAD = CONV_WIDTH // 2
CHUNK = 128
CONV_DIM = D_INNER + 2 * SSM_GROUPS * D_STATE
EPS = 1e-6

LANES = 128
MXU_DIM = 256
BF16_SUBLANES = 16
VMEM_LIMIT = 56 * 1024 * 1024

P_TILE = 1024
P_PLAIN = (0, 2, 4, 6)
P_CONV = (1, 3, 5, 7)
P_MAIN = (len(P_PLAIN) + len(P_CONV)) * P_TILE
W_IN_CONV_TILE0 = (D_FOURIER + D_INNER) // P_TILE
DT_PAD = LANES


def _params(sem):
    return pltpu.CompilerParams(dimension_semantics=sem, vmem_limit_bytes=VMEM_LIMIT)


def _rms(x, g):
    return x * lax.rsqrt(jnp.mean(x * x, axis=-1, keepdims=True) + EPS) * g


def _resident(shape):
    nd = len(shape)
    return pl.BlockSpec(shape, lambda *_: (0,) * nd, pipeline_mode=pl.Buffered(1))


def _ffn_body(x_ref, pre_ref, post_ref, win_ref, wout_ref, o_ref, h_ref, *, tf):
    x = x_ref[...]
    h_ref[...] = _rms(x, pre_ref[...]).astype(BF16)
    acc = None
    for c in range(D_FF // tf):
        h = h_ref[...]
        gate = jnp.dot(h, win_ref[:, c * tf:(c + 1) * tf], preferred_element_type=F32)
        up = jnp.dot(h, win_ref[:, D_FF + c * tf:D_FF + (c + 1) * tf], preferred_element_type=F32)
        act = (gate * jax.nn.sigmoid(gate) * up).astype(BF16)
        part = jnp.dot(act, wout_ref[c * tf:(c + 1) * tf, :], preferred_element_type=F32)
        acc = part if acc is None else acc + part
    o_ref[...] = x + 0.5 * _rms(acc, post_ref[...])


def _ffn_block(x, pre_g, post_g, w_in, w_out, *, tm=512, tf=256):
    t = x.shape[0]
    assert t % tm == 0 and D_FF % tf == 0
    return pl.pallas_call(
        functools.partial(_ffn_body, tf=tf),
        grid=(t // tm,),
        in_specs=[pl.BlockSpec((tm, D_MODEL), lambda i: (i, 0)),
                  _resident((1, D_MODEL)), _resident((1, D_MODEL)),
                  _resident((D_MODEL, 2 * D_FF)), _resident((D_FF, D_MODEL))],
        out_specs=pl.BlockSpec((tm, D_MODEL), lambda i: (i, 0)),
        out_shape=jax.ShapeDtypeStruct((t, D_MODEL), F32),
        scratch_shapes=[pltpu.VMEM((tm, D_MODEL), BF16)],
        compiler_params=_params(("parallel",)),
        name="ffn_block",
    )(x, pre_g, post_g, w_in, w_out)


FOURIER_L1 = MXU_DIM
SUBLANES = 8


def _fourier_tables(l):
    l1 = FOURIER_L1
    l2 = l // l1
    g = MXU_DIM // l2
    w = FOURIER_GROUP_WIDTH
    c = np.arange(w)
    ang = 2.0 * np.pi * np.outer(c, c) / w
    cs_w = np.concatenate([np.cos(ang), -np.sin(ang)], axis=0) / math.sqrt(w)
    n1 = np.arange(l1)
    ang1 = 2.0 * np.pi * np.outer(n1, n1) / l1
    cs1 = np.concatenate([np.cos(ang1), np.sin(ang1)], axis=0)
    k1 = np.arange(l1).reshape(l1 // g, 1, g, 1, 1)
    k2 = np.arange(l2).reshape(1, l2, 1, 1, 1)
    n2 = np.arange(l2).reshape(1, 1, 1, 1, l2)
    theta = 2.0 * np.pi * (n2 * k2 / l2 + n2 * k1 / l)
    eye = np.eye(g).reshape(1, 1, g, g, 1)

    def tiled_cols(m):
        m = m.reshape(l1 // g, l2 * g, g, l2 // SUBLANES, SUBLANES)
        return m.transpose(0, 1, 3, 2, 4).reshape(l1 // g, l2 * g, g * l2)

    gc = tiled_cols(np.cos(theta) * eye)
    gs = tiled_cols(np.sin(theta) * eye)
    g2 = np.concatenate([np.concatenate([gc, -gs], axis=2),
                         np.concatenate([gs, gc], axis=2)], axis=1) / math.sqrt(l)
    as_bf16 = lambda a: jnp.asarray(a, dtype=F32).astype(BF16)
    return as_bf16(cs_w), as_bf16(cs1), as_bf16(g2)


def _pack_bf16_pair(hi, lo):
    bits = lambda v: lax.bitcast_convert_type(v.astype(BF16).astype(F32), jnp.uint32)
    return bits(hi) | (bits(lo) >> 16)


def _unpack_bf16_pair(word):
    hi = lax.bitcast_convert_type(word & jnp.uint32(0xFFFF0000), F32).astype(BF16)
    lo = lax.bitcast_convert_type(word << 16, F32).astype(BF16)
    return hi, lo


def _fourier1_body(u_ref, cs1_ref, z_ref):
    l1 = FOURIER_L1
    slabs = u_ref.shape[0]
    for j in range(SUBLANES):
        pick = pl.ds(j, l1, stride=SUBLANES)
        u = jnp.concatenate([u_ref.at[s][pick, :] for s in range(slabs)], axis=1).astype(BF16)
        z = jnp.dot(cs1_ref[...], u, preferred_element_type=F32)
        word = _pack_bf16_pair(z[:l1], z[l1:])
        for s in range(slabs):
            z_ref.at[s][pick, :] = word[:, s * LANES:(s + 1) * LANES]


def _fourier2_body(z_ref, g2_ref, csw_ref, o_ref, *, tiles, g):
    w = FOURIER_GROUP_WIDTH
    half = MXU_DIM
    slabs = z_ref.shape[2]
    rows = g * SUBLANES
    for t in range(tiles):
        parts = [_unpack_bf16_pair(z_ref[0, :, s, t * rows:(t + 1) * rows, :].reshape(half, LANES))
                 for s in range(slabs)]
        z = jnp.concatenate([jnp.concatenate([p[0] for p in parts], axis=1),
                             jnp.concatenate([p[1] for p in parts], axis=1)], axis=0)
        v = jnp.dot(g2_ref[t], z, preferred_element_type=F32).astype(BF16)
        for s in range(z.shape[1] // w):
            cols = slice(s * w, (s + 1) * w)
            vv = jnp.concatenate([v[:half, cols], v[half:, cols]], axis=1)
            res = jnp.dot(vv, csw_ref[...], preferred_element_type=F32)
            o_ref[0, :, t * g:(t + 1) * g, cols] = res.reshape(o_ref.shape[1], g, w)


def _fourier_branch(u, l, *, slabs_per_step=8, tiles_per_step=4):
    b, nt, ns, rows, _ = u.shape
    c = ns * LANES
    l1 = FOURIER_L1
    l2 = l // l1
    g = MXU_DIM // l2
    assert nt * SUBLANES == l2 and rows == l1 * SUBLANES and MXU_DIM % l2 == 0 and g % SUBLANES == 0
    assert ns % slabs_per_step == 0
    csw, cs1, g2 = _fourier_tables(l)
    blk = pl.BlockSpec((None, None, slabs_per_step, rows, LANES), lambda i, j, k: (i, j, k, 0, 0))
    z = pl.pallas_call(
        _fourier1_body,
        grid=(b, nt, ns // slabs_per_step),
        in_specs=[blk, _resident(cs1.shape)],
        out_specs=blk,
        out_shape=jax.ShapeDtypeStruct(u.shape, jnp.uint32),
        compiler_params=_params(("parallel", "parallel", "parallel")),
        name="fourier_stage1",
    )(u, cs1)
    tiles = tiles_per_step
    assert (l1 // g) % tiles == 0
    y = pl.pallas_call(
        functools.partial(_fourier2_body, tiles=tiles, g=g),
        grid=(b, l1 // g // tiles),
        in_specs=[pl.BlockSpec((1, nt, ns, tiles * g * SUBLANES, LANES), lambda i, t: (i, 0, 0, t, 0)),
                  pl.BlockSpec((tiles, 2 * MXU_DIM, 2 * MXU_DIM), lambda i, t: (t, 0, 0)),
                  _resident(csw.shape)],
        out_specs=pl.BlockSpec((1, l2, tiles * g, c), lambda i, t: (i, 0, t, 0)),
        out_shape=jax.ShapeDtypeStruct((b, l2, l1, c), F32),
        compiler_params=_params(("parallel", "parallel")),
        name="fourier_stage2",
    )(z, g2, csw)
    return y.reshape(b, l, c)


CONV_HALO = BF16_SUBLANES
CONV_PHASES = 4


def _proj_main_body(xp_ref, x_ref, xn_ref, g_ref, wp_ref, wc_ref, cw_ref, cb_ref, wf_ref, wdt_ref,
                    o_ref, u_ref, dt_ref, h_ref, ext_ref, act_ref, *, tm, tiles_per_seq, l2):
    i, j = pl.program_id(0), pl.program_id(1)
    halo = CONV_HALO

    @pl.when(j == 0)
    def _():
        h_ref[0:halo, :] = _rms(xp_ref[...], g_ref[...]).astype(BF16)
        h_ref[halo:halo + tm, :] = _rms(x_ref[...], g_ref[...]).astype(BF16)
        h_ref[halo + tm:, :] = _rms(xn_ref[...], g_ref[...]).astype(BF16)

    n_pairs = len(P_CONV)

    @pl.when(j < n_pairs)
    def _():
        seq_first = (i % tiles_per_seq) == 0
        seq_last = (i % tiles_per_seq) == tiles_per_seq - 1
        prod = jnp.dot(h_ref[...], wc_ref[...], preferred_element_type=F32)
        n_slabs = P_TILE // LANES
        for s in range(n_slabs):
            ext_ref[s] = prod[:, s * LANES:(s + 1) * LANES]
        ext_ref[:, 0:halo, :] = jnp.where(seq_first, 0.0, ext_ref[:, 0:halo, :])
        ext_ref[:, halo + tm:, :] = jnp.where(seq_last, 0.0, ext_ref[:, halo + tm:, :])

        o_ref[:, :P_TILE] = jnp.dot(h_ref[halo:halo + tm, :], wp_ref[...],
                                    preferred_element_type=F32).astype(o_ref.dtype)
        n = tm // CONV_PHASES
        for s in range(n_slabs):
            lanes = slice(s * LANES, (s + 1) * LANES)
            ext, act = ext_ref.at[s], act_ref.at[s]
            taps = {m: ext[pl.ds(halo + m, n, stride=CONV_PHASES), :]
                    for m in range(-CONV_PAD, CONV_PHASES + CONV_PAD)}
            for ph in range(CONV_PHASES):
                acc = cb_ref[:, lanes]
                for k in range(CONV_WIDTH):
                    acc = acc + cw_ref[k:k + 1, lanes] * taps[ph + k - CONV_PAD]
                act[pl.ds(ph, n, stride=CONV_PHASES), :] = acc * jax.nn.sigmoid(acc)
        for s in range(n_slabs):
            o_ref[:, P_TILE + s * LANES:P_TILE + (s + 1) * LANES] = act_ref[s].astype(o_ref.dtype)

    @pl.when(j == n_pairs)
    def _():
        h = h_ref[halo:halo + tm, :]
        dt_ref[...] = jnp.dot(h, wdt_ref[...], preferred_element_type=F32)
        u = jnp.dot(h, wf_ref[...], preferred_element_type=F32)
        tn = u.shape[1]
        u = u.reshape(tm // l2, l2 // SUBLANES, SUBLANES, tn)
        for t in range(l2 // SUBLANES):
            ut = u[:, t].reshape(tm // l2 * SUBLANES, tn)
            for s in range(tn // LANES):
                u_ref[0, t, s] = ut[:, s * LANES:(s + 1) * LANES]


def _proj_main(x, g, w_plain, w_all, conv_tile0, w_dt, conv_w, conv_b, b, l, *, tm=1024):
    t, n = x.shape[0], 2 * w_plain.shape[1]
    halo = CONV_HALO
    pair = 2 * P_TILE
    n_pairs = len(P_CONV)
    c = D_FOURIER
    l1 = FOURIER_L1
    l2 = l // l1
    per_seq = l // tm
    assert t == b * l and t % tm == 0 and l % tm == 0 and n == P_MAIN and tm % halo == 0
    assert tm % l2 == 0 and l2 % SUBLANES == 0 and c % LANES == 0
    assert P_PLAIN == tuple(range(0, 2 * n_pairs, 2)) and P_CONV == tuple(range(1, 2 * n_pairs, 2))
    assert (conv_tile0 + n_pairs) * P_TILE <= w_all.shape[1]
    hb = tm // halo
    nhb = t // halo
    pj = lambda j: jnp.minimum(j, n_pairs - 1)
    return pl.pallas_call(
        functools.partial(_proj_main_body, tm=tm, tiles_per_seq=per_seq, l2=l2),
        grid=(t // tm, n_pairs + 1),
        in_specs=[pl.BlockSpec((halo, D_MODEL), lambda i, j: (jnp.maximum(i * hb - 1, 0), 0)),
                  pl.BlockSpec((tm, D_MODEL), lambda i, j: (i, 0)),
                  pl.BlockSpec((halo, D_MODEL), lambda i, j: (jnp.minimum((i + 1) * hb, nhb - 1), 0)),
                  pl.BlockSpec((1, D_MODEL), lambda i, j: (0, 0)),
                  pl.BlockSpec((D_MODEL, P_TILE), lambda i, j: (0, pj(j))),
                  pl.BlockSpec((D_MODEL, P_TILE), lambda i, j: (0, conv_tile0 + pj(j))),
                  pl.BlockSpec((CONV_WIDTH, P_TILE), lambda i, j: (0, pj(j))),
                  pl.BlockSpec((1, P_TILE), lambda i, j: (0, pj(j))),
                  pl.BlockSpec((D_MODEL, c), lambda i, j: (0, 0)),
                  pl.BlockSpec((D_MODEL, DT_PAD), lambda i, j: (0, 0))],
        out_specs=[pl.BlockSpec((tm, pair), lambda i, j: (i, pj(j))),
                   pl.BlockSpec((1, l2 // SUBLANES, c // LANES, tm // l2 * SUBLANES, LANES),
                                lambda i, j: (i // per_seq, 0, 0, i % per_seq, 0)),
                   pl.BlockSpec((tm, DT_PAD), lambda i, j: (i, 0))],
        out_shape=[jax.ShapeDtypeStruct((t, n), BF16),
                   jax.ShapeDtypeStruct((b, l2 // SUBLANES, c // LANES, l1 * SUBLANES, LANES), F32),
                   jax.ShapeDtypeStruct((t, DT_PAD), F32)],
        scratch_shapes=[pltpu.VMEM((tm + 2 * halo, D_MODEL), BF16),
                        pltpu.VMEM((P_TILE // LANES, tm + 2 * halo, LANES), F32),
                        pltpu.VMEM((P_TILE // LANES, tm, LANES), F32)],
        compiler_params=_params(("parallel", "arbitrary")),
        name="proj_main",
    )(x, x, x, g, w_plain, w_all, conv_w, conv_b, w_all, w_dt)


LOG2E = math.log2(math.e)

HEADS_PER_PAIR = LANES // SSM_HEAD_DIM
N_PAIRS = SSM_HEADS // HEADS_PER_PAIR
HEADS_PER_GROUP = SSM_HEADS // SSM_GROUPS
PAIRS_PER_GROUP = HEADS_PER_GROUP // HEADS_PER_PAIR


def _split3(a):
    hi = a.astype(BF16)
    r1 = a - hi.astype(F32)
    mid = r1.astype(BF16)
    lo = (r1 - mid.astype(F32)).astype(BF16)
    return hi, mid, lo


def _ssd_prep_body(dt_ref, bias_ref, alog_ref, cum_ref, rowb_ref, w_ref, *, chunks):
    q = CHUNK
    row = lax.broadcasted_iota(jnp.int32, (q, q), 0)
    col = lax.broadcasted_iota(jnp.int32, (q, q), 1)
    tri = jnp.concatenate([jnp.where(col <= row, 1.0, 0.0), jnp.where(col >= row, 1.0, 0.0)],
                          axis=0).astype(BF16)
    fwd_lane = lax.broadcasted_iota(jnp.int32, (q, LANES), 1) < SSM_HEADS
    neg_a = -jnp.exp(alog_ref[...]) * LOG2E
    for c in range(chunks):
        rows = slice(c * q, (c + 1) * q)
        dtv = jax.nn.softplus(dt_ref[0, rows, :] + bias_ref[...])
        a = dtv * neg_a
        both = sum(jnp.dot(tri, piece, preferred_element_type=F32) for piece in _split3(a))
        cum2 = jnp.where(fwd_lane, both[:q], both[q:])
        cend = jnp.where(fwd_lane[0:1], cum2[q - 1:q], cum2[0:1])
        cum_ref[0, rows, :] = cum2
        rowb_ref[0, rows, :] = (cum2 - jnp.log2(dtv)).T
        w_ref[0, rows, :] = (jnp.exp2(cend - cum2) * dtv).T


def _ssd_prep(dt_raw, bias, alog, *, chunks=4):
    b, l, _ = dt_raw.shape
    rows = chunks * CHUNK
    assert l % rows == 0
    spec = pl.BlockSpec((1, rows, DT_PAD), lambda n, c: (n, c, 0))
    vec = pl.BlockSpec((1, DT_PAD), lambda n, c: (0, 0))
    shape = jax.ShapeDtypeStruct((b, l, DT_PAD), F32)
    return pl.pallas_call(
        functools.partial(_ssd_prep_body, chunks=chunks),
        grid=(b, l // rows),
        in_specs=[spec, vec, vec],
        out_specs=[spec, spec, spec],
        out_shape=[shape, shape, shape],
        compiler_params=_params(("parallel", "parallel")),
        name="ssd_prep",
    )(dt_raw, bias, alog)


def _ssd_body(*refs, direction, final, cps):
    if final:
        x0_ref, x1_ref, b_ref, c_ref, cum_ref, rowb_ref, w_ref, yf_ref, dskip_ref, o_ref, s_ref = refs
    else:
        x0_ref, x1_ref, b_ref, c_ref, cum_ref, rowb_ref, w_ref, o_ref, s_ref = refs
    q = CHUNK
    pairs_per_tile = P_TILE // LANES

    @pl.when(pl.program_id(1) == 0)
    def _():
        s_ref[...] = jnp.zeros_like(s_ref)

    row = lax.broadcasted_iota(jnp.int32, (q, q), 0)
    col = lax.broadcasted_iota(jnp.int32, (q, q), 1)
    causal = (col <= row) if direction == 0 else (col >= row)
    end = q - 1 if direction == 0 else 0
    first_head = lax.broadcasted_iota(jnp.int32, (1, LANES), 1) < SSM_HEAD_DIM
    keep_bf = [jnp.where(first_head, 1.0, 0.0).astype(BF16),
               jnp.where(first_head, 0.0, 1.0).astype(BF16)]

    def group_prologue(grp, ck):
        gcols = slice(grp * D_STATE, (grp + 1) * D_STATE)
        rows = slice(ck * q, (ck + 1) * q)
        bg = b_ref[0, rows, gcols]
        cg = c_ref[0, rows, gcols]
        cb = lax.dot_general(cg, bg, (((1,), (1,)), ((), ())), preferred_element_type=F32)
        bg_t = bg.astype(F32).T
        first_pair = grp * PAIRS_PER_GROUP
        s_grp = jnp.concatenate([s_ref[first_pair + k].astype(BF16) for k in range(PAIRS_PER_GROUP)], axis=1)
        y_off = jnp.dot(cg, s_grp, preferred_element_type=F32)
        return cb, bg_t, y_off

    def pair_work(grp, ck, k, cb, bg_t, y_off):
        r0 = ck * q
        rows = slice(r0, r0 + q)
        pr = grp * PAIRS_PER_GROUP + k
        x_ref = x0_ref if pr < pairs_per_tile else x1_ref
        xp = x_ref[0, rows, (pr % pairs_per_tile) * LANES:(pr % pairs_per_tile + 1) * LANES]
        ms, bws, xs, cums = [], [], [], []
        for i in range(HEADS_PER_PAIR):
            hc = SSM_HEADS * direction + HEADS_PER_PAIR * pr + i
            colb = jnp.broadcast_to(cum_ref[0, rows, hc:hc + 1], (q, LANES))
            m = jnp.where(causal, jnp.exp2(colb - rowb_ref[0, r0 + hc:r0 + hc + 1, :]), 0.0) * cb
            ms.append(m.astype(BF16))
            xs.append(xp * keep_bf[i])
            bws.append((bg_t * w_ref[0, r0 + hc:r0 + hc + 1, :]).astype(BF16))
            cums.append(colb)
        x_split = jnp.concatenate(xs, axis=0)
        y_diag = jnp.dot(jnp.concatenate(ms, axis=1), x_split, preferred_element_type=F32)
        ecol = jnp.exp2(jnp.where(first_head, cums[0], cums[1]))
        y_pair = y_diag + y_off[:, k * LANES:(k + 1) * LANES] * ecol
        s_new = jnp.dot(jnp.concatenate(bws, axis=1), x_split, preferred_element_type=F32)
        s_ref[pr] = s_ref[pr] * ecol[end:end + 1, :] + s_new
        lanes = slice(pr * LANES, (pr + 1) * LANES)
        if final:
            y_pair = y_pair + yf_ref[0, rows, lanes].astype(F32) + xp.astype(F32) * dskip_ref[:, lanes]
        o_ref[0, rows, lanes] = y_pair.astype(o_ref.dtype)

    order = range(cps) if direction == 0 else range(cps - 1, -1, -1)
    for g0 in range(0, SSM_GROUPS, 2):
        for ck in order:
            pro = [group_prologue(g0 + d, ck) for d in range(2)]
            for k in range(PAIRS_PER_GROUP):
                for d in range(2):
                    pair_work(g0 + d, ck, k, *pro[d])


def _ssd_pass(p_main, prep, direction, final_inputs=None, *, cps=8):
    b, l, _ = p_main.shape
    rows = cps * CHUNK
    nc = l // rows
    assert l % rows == 0 and D_INNER == 2 * P_TILE and SSM_GROUPS * D_STATE == P_TILE
    chunk = (lambda c: c) if direction == 0 else (lambda c: nc - 1 - c)
    tile = lambda k: pl.BlockSpec((1, rows, P_TILE), lambda n, c: (n, chunk(c), k))
    small = pl.BlockSpec((1, rows, DT_PAD), lambda n, c: (n, chunk(c), 0))
    in_specs = [tile(P_CONV[0]), tile(P_CONV[1]), tile(P_CONV[2]), tile(P_CONV[3]), small, small, small]
    args = [p_main, p_main, p_main, p_main, *prep]
    scratch = [pltpu.VMEM((N_PAIRS, D_STATE, LANES), F32)]
    final = final_inputs is not None
    if final:
        y_fwd, dskip = final_inputs
        in_specs += [pl.BlockSpec((1, rows, D_INNER), lambda n, c: (n, chunk(c), 0)),
                     pl.BlockSpec((1, D_INNER), lambda n, c: (0, 0))]
        args += [y_fwd, dskip]
    return pl.pallas_call(
        functools.partial(_ssd_body, direction=direction, final=final, cps=cps),
        grid=(b, nc),
        in_specs=in_specs,
        out_specs=pl.BlockSpec((1, rows, D_INNER), lambda n, c: (n, chunk(c), 0)),
        out_shape=jax.ShapeDtypeStruct((b, l, D_INNER), BF16),
        scratch_shapes=scratch,
        compiler_params=_params(("parallel", "arbitrary")),
        name="ssd_bwd" if final else "ssd_fwd",
    )(*args)


def _merge_body(x_ref, yf_ref, y_ref, z0_ref, z1_ref, gf_ref, gm_ref, ng_ref, wf_ref, wm_ref, wo_ref,
                post_ref, o_ref, *, tk):
    f = jnp.dot(yf_ref[...].astype(BF16), wf_ref[...], preferred_element_type=F32)
    acc, ssq = None, None
    for c in range(D_INNER // tk):
        cols = slice(c * tk, (c + 1) * tk)
        z_ref = z0_ref if c * tk < P_TILE else z1_ref
        z = z_ref[:, (c * tk) % P_TILE:(c * tk) % P_TILE + tk].astype(F32)
        gated = y_ref[:, cols].astype(F32) * (z * jax.nn.sigmoid(z))
        sq = jnp.sum(gated * gated, axis=-1, keepdims=True)
        part = jnp.dot((gated * ng_ref[:, cols]).astype(BF16), wm_ref[cols, :], preferred_element_type=F32)
        acc = part if acc is None else acc + part
        ssq = sq if ssq is None else ssq + sq
    m = acc * lax.rsqrt(ssq * (1.0 / D_INNER) + EPS)
    merged = (jax.nn.sigmoid(gf_ref[...].astype(F32)) * f
              + jax.nn.sigmoid(gm_ref[...].astype(F32)) * m).astype(BF16)
    out = jnp.dot(merged, wo_ref[...], preferred_element_type=F32)
    o_ref[...] = x_ref[...] + _rms(out, post_ref[...])


def _merge_block(x, yf, y_ssd, p_main, norm_g, w_f, w_m, w_o, post_g, *, tm=512, tk=256):
    t = x.shape[0]
    assert t % tm == 0 and P_TILE == D_MODEL and P_TILE % tk == 0
    tile = lambda k: pl.BlockSpec((tm, P_TILE), lambda i: (i, k))
    return pl.pallas_call(
        functools.partial(_merge_body, tk=tk),
        grid=(t // tm,),
        in_specs=[pl.BlockSpec((tm, D_MODEL), lambda i: (i, 0)),
                  pl.BlockSpec((tm, D_FOURIER), lambda i: (i, 0)),
                  pl.BlockSpec((tm, D_INNER), lambda i: (i, 0)),
                  tile(P_PLAIN[0]), tile(P_PLAIN[1]), tile(P_PLAIN[2]), tile(P_PLAIN[3]),
                  _resident((1, D_INNER)),
                  _resident((D_FOURIER, D_MODEL)), _resident((D_INNER, D_MODEL)),
                  _resident((D_MODEL, D_MODEL)), _resident((1, D_MODEL))],
        out_specs=pl.BlockSpec((tm, D_MODEL), lambda i: (i, 0)),
        out_shape=jax.ShapeDtypeStruct((t, D_MODEL), F32),
        compiler_params=_params(("parallel",)),
        name="merge_out",
    )(x, yf, y_ssd, p_main, p_main, p_main, p_main, norm_g, w_f, w_m, w_o, post_g)


def _prepare(ffn1_pre_g, ffn1_post_g, ffn1_w_in, ffn1_w_out, mix_pre_g, mix_post_g, w_in, conv_w,
             conv_b, dt_bias, a_log, d_skip, ssm_norm_g, w_branch_f, w_branch_m, w_out,
             ffn2_pre_g, ffn2_post_g, ffn2_w_in, ffn2_w_out):
    row = lambda v: v.reshape(1, -1).astype(F32)
    off_f, off_z = D_FOURIER, D_FOURIER + D_INNER
    off_xbc = off_z + CONV_DIM
    off_dt = off_xbc + 2 * SSM_HEADS
    assert off_z == W_IN_CONV_TILE0 * P_TILE
    w_all = w_in.astype(BF16)
    w_plain = jnp.concatenate([w_all[:, off_f:off_z], w_all[:, off_dt:]], axis=1)
    pad = DT_PAD - 2 * SSM_HEADS
    w_dt = jnp.pad(w_all[:, off_xbc:off_dt], ((0, 0), (0, pad)))
    return dict(
        ffn1=(row(ffn1_pre_g), row(ffn1_post_g), ffn1_w_in.astype(BF16), ffn1_w_out.astype(BF16)),
        ffn2=(row(ffn2_pre_g), row(ffn2_post_g), ffn2_w_in.astype(BF16), ffn2_w_out.astype(BF16)),
        mix_pre=row(mix_pre_g), mix_post=row(mix_post_g),
        w_all=w_all, w_plain=w_plain, w_dt=w_dt,
        conv_w=conv_w.astype(F32), conv_b=row(conv_b),
        dt_bias=jnp.pad(row(dt_bias), ((0, 0), (0, pad))),
        a_log=jnp.pad(row(a_log), ((0, 0), (0, pad))),
        d_skip=row(jnp.repeat(d_skip, SSM_HEAD_DIM)), norm_g=row(ssm_norm_g),
        w_f=w_branch_f.astype(BF16), w_m=w_branch_m.astype(BF16), w_o=w_out.astype(BF16))


def _encoder_layer(x, p):
    b, l, d = x.shape
    t = b * l
    x = _ffn_block(x.reshape(t, d), *p["ffn1"])
    p_main, u_f, dt_raw = _proj_main(x, p["mix_pre"], p["w_plain"], p["w_all"], W_IN_CONV_TILE0, p["w_dt"],
                                     p["conv_w"], p["conv_b"], b, l)
    p_main = p_main.reshape(b, l, P_MAIN)
    dt_raw = dt_raw.reshape(b, l, DT_PAD)
    yf = _fourier_branch(u_f, l)
    prep = _ssd_prep(dt_raw, p["dt_bias"], p["a_log"])
    y_fwd = _ssd_pass(p_main, prep, 0)
    y_ssd = _ssd_pass(p_main, prep, 1, final_inputs=(y_fwd, p["d_skip"]))
    x = _merge_block(x, yf.reshape(t, D_FOURIER), y_ssd.reshape(t, D_INNER), p_main.reshape(t, P_MAIN),
                     p["norm_g"], p["w_f"], p["w_m"], p["w_o"], p["mix_post"])
    x = _ffn_block(x, *p["ffn2"])
    return x.reshape(b, l, d)


def kernel(x_prompt, x_sample, ffn1_pre_g, ffn1_post_g, ffn1_w_in, ffn1_w_out, mix_pre_g, mix_post_g, w_in, conv_w, conv_b, dt_bias, a_log, d_skip, ssm_norm_g, w_branch_f, w_branch_m, w_out, ffn2_pre_g, ffn2_post_g, ffn2_w_in, ffn2_w_out):
    layers = (ffn1_pre_g, ffn1_post_g, ffn1_w_in, ffn1_w_out, mix_pre_g, mix_post_g, w_in, conv_w,
              conv_b, dt_bias, a_log, d_skip, ssm_norm_g, w_branch_f, w_branch_m, w_out,
              ffn2_pre_g, ffn2_post_g, ffn2_w_in, ffn2_w_out)
    y_prompt, y_sample = x_prompt, x_sample
    for layer in range(ffn1_pre_g.shape[0]):
        p = _prepare(*(a[layer] for a in layers))
        y_prompt = _encoder_layer(y_prompt, p)
        y_sample = _encoder_layer(y_sample, p)
    return (y_prompt, y_sample)
```

```python
import functools
import math

import numpy as np
import jax
import jax.numpy as jnp
from jax import lax
from jax.experimental import pallas as pl
from jax.experimental.pallas import tpu as pltpu

F32 = jnp.float32
BF16 = jnp.bfloat16

D_MODEL = 1024
D_FF = 2816
FOURIER_GROUP_WIDTH = 256
D_FOURIER = 1024
SSM_HEADS = 32
SSM_HEAD_DIM = 64
D_INNER = SSM_HEADS * SSM_HEAD_DIM
SSM_GROUPS = 8
D_STATE = 128
CONV_WIDTH = 5
CONV_PAD = CONV_WIDTH // 2
CHUNK = 128
CONV_DIM = D_INNER + 2 * SSM_GROUPS * D_STATE
EPS = 1e-6

LANES = 128
MXU_DIM = 256
BF16_SUBLANES = 16
VMEM_LIMIT = 56 * 1024 * 1024

P_TILE = 1024
P_PLAIN = (0, 2, 4, 6)
P_CONV = (1, 3, 5, 7)
P_MAIN = (len(P_PLAIN) + len(P_CONV)) * P_TILE
W_IN_CONV_TILE0 = (D_FOURIER + D_INNER) // P_TILE
DT_PAD = LANES


def _params(sem):
    return pltpu.CompilerParams(dimension_semantics=sem, vmem_limit_bytes=VMEM_LIMIT)


def _rms(x, g):
    return x * lax.rsqrt(jnp.mean(x * x, axis=-1, keepdims=True) + EPS) * g


def _resident(shape):
    nd = len(shape)
    return pl.BlockSpec(shape, lambda *_: (0,) * nd, pipeline_mode=pl.Buffered(1))


def _ffn_body(x_ref, pre_ref, post_ref, win_ref, wout_ref, o_ref, h_ref, *, tf):
    x = x_ref[...]
    h_ref[...] = _rms(x, pre_ref[...]).astype(BF16)
    acc = None
    for c in range(D_FF // tf):
        h = h_ref[...]
        gate = jnp.dot(h, win_ref[:, c * tf:(c + 1) * tf], preferred_element_type=F32)
        up = jnp.dot(h, win_ref[:, D_FF + c * tf:D_FF + (c + 1) * tf], preferred_element_type=F32)
        act = (gate * jax.nn.sigmoid(gate) * up).astype(BF16)
        part = jnp.dot(act, wout_ref[c * tf:(c + 1) * tf, :], preferred_element_type=F32)
        acc = part if acc is None else acc + part
    o_ref[...] = x + 0.5 * _rms(acc, post_ref[...])


def _ffn_block(x, pre_g, post_g, w_in, w_out, *, tm=512, tf=256):
    t = x.shape[0]
    assert t % tm == 0 and D_FF % tf == 0
    return pl.pallas_call(
        functools.partial(_ffn_body, tf=tf),
        grid=(t // tm,),
        in_specs=[pl.BlockSpec((tm, D_MODEL), lambda i: (i, 0)),
                  _resident((1, D_MODEL)), _resident((1, D_MODEL)),
                  _resident((D_MODEL, 2 * D_FF)), _resident((D_FF, D_MODEL))],
        out_specs=pl.BlockSpec((tm, D_MODEL), lambda i: (i, 0)),
        out_shape=jax.ShapeDtypeStruct((t, D_MODEL), F32),
        scratch_shapes=[pltpu.VMEM((tm, D_MODEL), BF16)],
        compiler_params=_params(("parallel",)),
        name="ffn_block",
    )(x, pre_g, post_g, w_in, w_out)


FOURIER_L1 = MXU_DIM
SUBLANES = 8


def _fourier_tables(l):
    l1 = FOURIER_L1
    l2 = l // l1
    g = MXU_DIM // l2
    w = FOURIER_GROUP_WIDTH
    c = np.arange(w)
    ang = 2.0 * np.pi * np.outer(c, c) / w
    cs_w = np.concatenate([np.cos(ang), -np.sin(ang)], axis=0) / math.sqrt(w)
    n1 = np.arange(l1)
    ang1 = 2.0 * np.pi * np.outer(n1, n1) / l1
    cs1 = np.concatenate([np.cos(ang1), np.sin(ang1)], axis=0)
    k1 = np.arange(l1).reshape(l1 // g, 1, g, 1, 1)
    k2 = np.arange(l2).reshape(1, l2, 1, 1, 1)
    n2 = np.arange(l2).reshape(1, 1, 1, 1, l2)
    theta = 2.0 * np.pi * (n2 * k2 / l2 + n2 * k1 / l)
    eye = np.eye(g).reshape(1, 1, g, g, 1)

    def tiled_cols(m):
        m = m.reshape(l1 // g, l2 * g, g, l2 // SUBLANES, SUBLANES)
        return m.transpose(0, 1, 3, 2, 4).reshape(l1 // g, l2 * g, g * l2)

    gc = tiled_cols(np.cos(theta) * eye)
    gs = tiled_cols(np.sin(theta) * eye)
    g2 = np.concatenate([np.concatenate([gc, -gs], axis=2),
                         np.concatenate([gs, gc], axis=2)], axis=1) / math.sqrt(l)
    as_bf16 = lambda a: jnp.asarray(a, dtype=F32).astype(BF16)
    return as_bf16(cs_w), as_bf16(cs1), as_bf16(g2)


def _pack_bf16_pair(hi, lo):
    bits = lambda v: lax.bitcast_convert_type(v.astype(BF16).astype(F32), jnp.uint32)
    return bits(hi) | (bits(lo) >> 16)


def _unpack_bf16_pair(word):
    hi = lax.bitcast_convert_type(word & jnp.uint32(0xFFFF0000), F32).astype(BF16)
    lo = lax.bitcast_convert_type(word << 16, F32).astype(BF16)
    return hi, lo


def _fourier1_body(u_ref, cs1_ref, z_ref):
    l1 = FOURIER_L1
    slabs = u_ref.shape[0]
    for j in range(SUBLANES):
        pick = pl.ds(j, l1, stride=SUBLANES)
        u = jnp.concatenate([u_ref.at[s][pick, :] for s in range(slabs)], axis=1).astype(BF16)
        z = jnp.dot(cs1_ref[...], u, preferred_element_type=F32)
        word = _pack_bf16_pair(z[:l1], z[l1:])
        for s in range(slabs):
            z_ref.at[s][pick, :] = word[:, s * LANES:(s + 1) * LANES]


def _fourier2_body(z_ref, g2_ref, csw_ref, o_ref, *, tiles, g):
    w = FOURIER_GROUP_WIDTH
    half = MXU_DIM
    slabs = z_ref.shape[2]
    rows = g * SUBLANES
    for t in range(tiles):
        parts = [_unpack_bf16_pair(z_ref[0, :, s, t * rows:(t + 1) * rows, :].reshape(half, LANES))
                 for s in range(slabs)]
        z = jnp.concatenate([jnp.concatenate([p[0] for p in parts], axis=1),
                             jnp.concatenate([p[1] for p in parts], axis=1)], axis=0)
        v = jnp.dot(g2_ref[t], z, preferred_element_type=F32).astype(BF16)
        for s in range(z.shape[1] // w):
            cols = slice(s * w, (s + 1) * w)
            vv = jnp.concatenate([v[:half, cols], v[half:, cols]], axis=1)
            res = jnp.dot(vv, csw_ref[...], preferred_element_type=F32)
            o_ref[0, :, t * g:(t + 1) * g, cols] = res.reshape(o_ref.shape[1], g, w)


def _fourier_branch(u, l, *, slabs_per_step=8, tiles_per_step=4):
    b, nt, ns, rows, _ = u.shape
    c = ns * LANES
    l1 = FOURIER_L1
    l2 = l // l1
    g = MXU_DIM // l2
    assert nt * SUBLANES == l2 and rows == l1 * SUBLANES and MXU_DIM % l2 == 0 and g % SUBLANES == 0
    assert ns % slabs_per_step == 0
    csw, cs1, g2 = _fourier_tables(l)
    blk = pl.BlockSpec((None, None, slabs_per_step, rows, LANES), lambda i, j, k: (i, j, k, 0, 0))
    z = pl.pallas_call(
        _fourier1_body,
        grid=(b, nt, ns // slabs_per_step),
        in_specs=[blk, _resident(cs1.shape)],
        out_specs=blk,
        out_shape=jax.ShapeDtypeStruct(u.shape, jnp.uint32),
        compiler_params=_params(("parallel", "parallel", "parallel")),
        name="fourier_stage1",
    )(u, cs1)
    tiles = tiles_per_step
    assert (l1 // g) % tiles == 0
    y = pl.pallas_call(
        functools.partial(_fourier2_body, tiles=tiles, g=g),
        grid=(b, l1 // g // tiles),
        in_specs=[pl.BlockSpec((1, nt, ns, tiles * g * SUBLANES, LANES), lambda i, t: (i, 0, 0, t, 0)),
                  pl.BlockSpec((tiles, 2 * MXU_DIM, 2 * MXU_DIM), lambda i, t: (t, 0, 0)),
                  _resident(csw.shape)],
        out_specs=pl.BlockSpec((1, l2, tiles * g, c), lambda i, t: (i, 0, t, 0)),
        out_shape=jax.ShapeDtypeStruct((b, l2, l1, c), F32),
        compiler_params=_params(("parallel", "parallel")),
        name="fourier_stage2",
    )(z, g2, csw)
    return y.reshape(b, l, c)


CONV_HALO = BF16_SUBLANES
CONV_PHASES = 4


def _proj_main_body(xp_ref, x_ref, xn_ref, g_ref, wp_ref, wc_ref, cw_ref, cb_ref, wf_ref, wdt_ref,
                    o_ref, u_ref, dt_ref, h_ref, ext_ref, act_ref, *, tm, tiles_per_seq, l2):
    i, j = pl.program_id(0), pl.program_id(1)
    halo = CONV_HALO

    @pl.when(j == 0)
    def _():
        h_ref[0:halo, :] = _rms(xp_ref[...], g_ref[...]).astype(BF16)
        h_ref[halo:halo + tm, :] = _rms(x_ref[...], g_ref[...]).astype(BF16)
        h_ref[halo + tm:, :] = _rms(xn_ref[...], g_ref[...]).astype(BF16)

    n_pairs = len(P_CONV)

    @pl.when(j < n_pairs)
    def _():
        seq_first = (i % tiles_per_seq) == 0
        seq_last = (i % tiles_per_seq) == tiles_per_seq - 1
        wcols = pl.ds(pl.multiple_of(j * P_TILE, P_TILE), P_TILE)
        prod = jnp.dot(h_ref[...], wc_ref[:, wcols], preferred_element_type=F32)
        n_slabs = P_TILE // LANES
        for s in range(n_slabs):
            ext_ref[s] = prod[:, s * LANES:(s + 1) * LANES]
        ext_ref[:, 0:halo, :] = jnp.where(seq_first, 0.0, ext_ref[:, 0:halo, :])
        ext_ref[:, halo + tm:, :] = jnp.where(seq_last, 0.0, ext_ref[:, halo + tm:, :])

        o_ref[:, :P_TILE] = jnp.dot(h_ref[halo:halo + tm, :], wp_ref[...],
                                    preferred_element_type=F32).astype(o_ref.dtype)
        n = tm // CONV_PHASES
        for s in range(n_slabs):
            lanes = slice(s * LANES, (s + 1) * LANES)
            ext, act = ext_ref.at[s], act_ref.at[s]
            taps = {m: ext[pl.ds(halo + m, n, stride=CONV_PHASES), :]
                    for m in range(-CONV_PAD, CONV_PHASES + CONV_PAD)}
            for ph in range(CONV_PHASES):
                acc = cb_ref[:, lanes]
                for k in range(CONV_WIDTH):
                    acc = acc + cw_ref[k:k + 1, lanes] * taps[ph + k - CONV_PAD]
                act[pl.ds(ph, n, stride=CONV_PHASES), :] = acc * jax.nn.sigmoid(acc)
        for s in range(n_slabs):
            o_ref[:, P_TILE + s * LANES:P_TILE + (s + 1) * LANES] = act_ref[s].astype(o_ref.dtype)

    @pl.when(j == n_pairs)
    def _():
        h = h_ref[halo:halo + tm, :]
        dt_ref[...] = jnp.dot(h, wdt_ref[...], preferred_element_type=F32)
        u = jnp.dot(h, wf_ref[...], preferred_element_type=F32)
        tn = u.shape[1]
        u = u.reshape(tm // l2, l2 // SUBLANES, SUBLANES, tn)
        for t in range(l2 // SUBLANES):
            ut = u[:, t].reshape(tm // l2 * SUBLANES, tn)
            for s in range(tn // LANES):
                u_ref[0, t, s] = ut[:, s * LANES:(s + 1) * LANES]


def _proj_main(x, g, w_plain, w_all, conv_tile0, w_dt, conv_w, conv_b, b, l, *, tm=1024):
    t, n = x.shape[0], 2 * w_plain.shape[1]
    halo = CONV_HALO
    pair = 2 * P_TILE
    n_pairs = len(P_CONV)
    c = D_FOURIER
    l1 = FOURIER_L1
    l2 = l // l1
    per_seq = l // tm
    assert t == b * l and t % tm == 0 and l % tm == 0 and n == P_MAIN and tm % halo == 0
    assert tm % l2 == 0 and l2 % SUBLANES == 0 and c % LANES == 0
    assert P_PLAIN == tuple(range(0, 2 * n_pairs, 2)) and P_CONV == tuple(range(1, 2 * n_pairs, 2))
    assert (conv_tile0 + n_pairs) * P_TILE <= w_all.shape[1]
    hb = tm // halo
    nhb = t // halo
    pj = lambda j: jnp.minimum(j, n_pairs - 1)
    return pl.pallas_call(
        functools.partial(_proj_main_body, tm=tm, tiles_per_seq=per_seq, l2=l2),
        grid=(t // tm, n_pairs + 1),
        in_specs=[pl.BlockSpec((halo, D_MODEL), lambda i, j: (jnp.maximum(i * hb - 1, 0), 0)),
                  pl.BlockSpec((tm, D_MODEL), lambda i, j: (i, 0)),
                  pl.BlockSpec((halo, D_MODEL), lambda i, j: (jnp.minimum((i + 1) * hb, nhb - 1), 0)),
                  pl.BlockSpec((1, D_MODEL), lambda i, j: (0, 0)),
                  pl.BlockSpec((D_MODEL, P_TILE), lambda i, j: (0, pj(j))),
                  _resident((D_MODEL, n_pairs * P_TILE)),
                  pl.BlockSpec((CONV_WIDTH, P_TILE), lambda i, j: (0, pj(j))),
                  pl.BlockSpec((1, P_TILE), lambda i, j: (0, pj(j))),
                  _resident((D_MODEL, c)), _resident((D_MODEL, DT_PAD))],
        out_specs=[pl.BlockSpec((tm, pair), lambda i, j: (i, pj(j))),
                   pl.BlockSpec((1, l2 // SUBLANES, c // LANES, tm // l2 * SUBLANES, LANES),
                                lambda i, j: (i // per_seq, 0, 0, i % per_seq, 0)),
                   pl.BlockSpec((tm, DT_PAD), lambda i, j: (i, 0))],
        out_shape=[jax.ShapeDtypeStruct((t, n), BF16),
                   jax.ShapeDtypeStruct((b, l2 // SUBLANES, c // LANES, l1 * SUBLANES, LANES), F32),
                   jax.ShapeDtypeStruct((t, DT_PAD), F32)],
        scratch_shapes=[pltpu.VMEM((tm + 2 * halo, D_MODEL), BF16),
                        pltpu.VMEM((P_TILE // LANES, tm + 2 * halo, LANES), F32),
                        pltpu.VMEM((P_TILE // LANES, tm, LANES), F32)],
        compiler_params=_params(("parallel", "arbitrary")),
        name="proj_main",
    )(x, x, x, g, w_plain, w_all[:, conv_tile0 * P_TILE:(conv_tile0 + n_pairs) * P_TILE], conv_w, conv_b, w_all, w_dt)


LOG2E = math.log2(math.e)

HEADS_PER_PAIR = LANES // SSM_HEAD_DIM
N_PAIRS = SSM_HEADS // HEADS_PER_PAIR
HEADS_PER_GROUP = SSM_HEADS // SSM_GROUPS
PAIRS_PER_GROUP = HEADS_PER_GROUP // HEADS_PER_PAIR


def _split3(a):
    hi = a.astype(BF16)
    r1 = a - hi.astype(F32)
    mid = r1.astype(BF16)
    lo = (r1 - mid.astype(F32)).astype(BF16)
    return hi, mid, lo


def _ssd_prep_body(dt_ref, bias_ref, alog_ref, cum_ref, rowb_ref, w_ref, *, chunks):
    q = CHUNK
    row = lax.broadcasted_iota(jnp.int32, (q, q), 0)
    col = lax.broadcasted_iota(jnp.int32, (q, q), 1)
    tri = jnp.concatenate([jnp.where(col <= row, 1.0, 0.0), jnp.where(col >= row, 1.0, 0.0)],
                          axis=0).astype(BF16)
    fwd_lane = lax.broadcasted_iota(jnp.int32, (q, LANES), 1) < SSM_HEADS
    neg_a = -jnp.exp(alog_ref[...]) * LOG2E
    for c in range(chunks):
        rows = slice(c * q, (c + 1) * q)
        dtv = jax.nn.softplus(dt_ref[0, rows, :] + bias_ref[...])
        a = dtv * neg_a
        both = sum(jnp.dot(tri, piece, preferred_element_type=F32) for piece in _split3(a))
        cum2 = jnp.where(fwd_lane, both[:q], both[q:])
        cend = jnp.where(fwd_lane[0:1], cum2[q - 1:q], cum2[0:1])
        cum_ref[0, rows, :] = cum2
        rowb_ref[0, rows, :] = (cum2 - jnp.log2(dtv)).T
        w_ref[0, rows, :] = (jnp.exp2(cend - cum2) * dtv).T


def _ssd_prep(dt_raw, bias, alog, *, chunks=4):
    b, l, _ = dt_raw.shape
    rows = chunks * CHUNK
    assert l % rows == 0
    spec = pl.BlockSpec((1, rows, DT_PAD), lambda n, c: (n, c, 0))
    vec = pl.BlockSpec((1, DT_PAD), lambda n, c: (0, 0))
    shape = jax.ShapeDtypeStruct((b, l, DT_PAD), F32)
    return pl.pallas_call(
        functools.partial(_ssd_prep_body, chunks=chunks),
        grid=(b, l // rows),
        in_specs=[spec, vec, vec],
        out_specs=[spec, spec, spec],
        out_shape=[shape, shape, shape],
        compiler_params=_params(("parallel", "parallel")),
        name="ssd_prep",
    )(dt_raw, bias, alog)


def _ssd_body(*refs, direction, final, cps):
    if final:
        x0_ref, x1_ref, b_ref, c_ref, cum_ref, rowb_ref, w_ref, yf_ref, dskip_ref, o_ref, s_ref = refs
    else:
        x0_ref, x1_ref, b_ref, c_ref, cum_ref, rowb_ref, w_ref, o_ref, s_ref = refs
    q = CHUNK
    pairs_per_tile = P_TILE // LANES

    @pl.when(pl.program_id(1) == 0)
    def _():
        s_ref[...] = jnp.zeros_like(s_ref)

    row = lax.broadcasted_iota(jnp.int32, (q, q), 0)
    col = lax.broadcasted_iota(jnp.int32, (q, q), 1)
    causal = (col <= row) if direction == 0 else (col >= row)
    end = q - 1 if direction == 0 else 0
    first_head = lax.broadcasted_iota(jnp.int32, (1, LANES), 1) < SSM_HEAD_DIM
    keep_bf = [jnp.where(first_head, 1.0, 0.0).astype(BF16),
               jnp.where(first_head, 0.0, 1.0).astype(BF16)]

    def group_prologue(grp, ck):
        gcols = slice(grp * D_STATE, (grp + 1) * D_STATE)
        rows = slice(ck * q, (ck + 1) * q)
        bg = b_ref[0, rows, gcols]
        cg = c_ref[0, rows, gcols]
        cb = lax.dot_general(cg, bg, (((1,), (1,)), ((), ())), preferred_element_type=F32)
        bg_t = bg.astype(F32).T
        first_pair = grp * PAIRS_PER_GROUP
        s_grp = jnp.concatenate([s_ref[first_pair + k].astype(BF16) for k in range(PAIRS_PER_GROUP)], axis=1)
        y_off = jnp.dot(cg, s_grp, preferred_element_type=F32)
        return cb, bg_t, y_off

    def pair_work(grp, ck, k, cb, bg_t, y_off):
        r0 = ck * q
        rows = slice(r0, r0 + q)
        pr = grp * PAIRS_PER_GROUP + k
        x_ref = x0_ref if pr < pairs_per_tile else x1_ref
        xp = x_ref[0, rows, (pr % pairs_per_tile) * LANES:(pr % pairs_per_tile + 1) * LANES]
        ms, bws, xs, cums = [], [], [], []
        for i in range(HEADS_PER_PAIR):
            hc = SSM_HEADS * direction + HEADS_PER_PAIR * pr + i
            colb = jnp.broadcast_to(cum_ref[0, rows, hc:hc + 1], (q, LANES))
            m = jnp.where(causal, jnp.exp2(colb - rowb_ref[0, r0 + hc:r0 + hc + 1, :]), 0.0) * cb
            ms.append(m.astype(BF16))
            xs.append(xp * keep_bf[i])
            bws.append((bg_t * w_ref[0, r0 + hc:r0 + hc + 1, :]).astype(BF16))
            cums.append(colb)
        x_split = jnp.concatenate(xs, axis=0)
        y_diag = jnp.dot(jnp.concatenate(ms, axis=1), x_split, preferred_element_type=F32)
        ecol = jnp.exp2(jnp.where(first_head, cums[0], cums[1]))
        y_pair = y_diag + y_off[:, k * LANES:(k + 1) * LANES] * ecol
        s_new = jnp.dot(jnp.concatenate(bws, axis=1), x_split, preferred_element_type=F32)
        s_ref[pr] = s_ref[pr] * ecol[end:end + 1, :] + s_new
        lanes = slice(pr * LANES, (pr + 1) * LANES)
        if final:
            y_pair = y_pair + yf_ref[0, rows, lanes].astype(F32) + xp.astype(F32) * dskip_ref[:, lanes]
        o_ref[0, rows, lanes] = y_pair.astype(o_ref.dtype)

    order = range(cps) if direction == 0 else range(cps - 1, -1, -1)
    for g0 in range(0, SSM_GROUPS, 2):
        for ck in order:
            pro = [group_prologue(g0 + d, ck) for d in range(2)]
            for k in range(PAIRS_PER_GROUP):
                for d in range(2):
                    pair_work(g0 + d, ck, k, *pro[d])


def _ssd_pass(p_main, prep, direction, final_inputs=None, *, cps=8):
    b, l, _ = p_main.shape
    rows = cps * CHUNK
    nc = l // rows
    assert l % rows == 0 and D_INNER == 2 * P_TILE and SSM_GROUPS * D_STATE == P_TILE
    chunk = (lambda c: c) if direction == 0 else (lambda c: nc - 1 - c)
    tile = lambda k: pl.BlockSpec((1, rows, P_TILE), lambda n, c: (n, chunk(c), k))
    small = pl.BlockSpec((1, rows, DT_PAD), lambda n, c: (n, chunk(c), 0))
    in_specs = [tile(P_CONV[0]), tile(P_CONV[1]), tile(P_CONV[2]), tile(P_CONV[3]), small, small, small]
    args = [p_main, p_main, p_main, p_main, *prep]
    scratch = [pltpu.VMEM((N_PAIRS, D_STATE, LANES), F32)]
    final = final_inputs is not None
    if final:
        y_fwd, dskip = final_inputs
        in_specs += [pl.BlockSpec((1, rows, D_INNER), lambda n, c: (n, chunk(c), 0)),
                     pl.BlockSpec((1, D_INNER), lambda n, c: (0, 0))]
        args += [y_fwd, dskip]
    return pl.pallas_call(
        functools.partial(_ssd_body, direction=direction, final=final, cps=cps),
        grid=(b, nc),
        in_specs=in_specs,
        out_specs=pl.BlockSpec((1, rows, D_INNER), lambda n, c: (n, chunk(c), 0)),
        out_shape=jax.ShapeDtypeStruct((b, l, D_INNER), BF16),
        scratch_shapes=scratch,
        compiler_params=_params(("parallel", "arbitrary")),
        name="ssd_bwd" if final else "ssd_fwd",
    )(*args)


def _merge_body(x_ref, yf_ref, y_ref, z0_ref, z1_ref, gf_ref, gm_ref, ng_ref, wf_ref, wm_ref, wo_ref,
                post_ref, o_ref, *, tk):
    f = jnp.dot(yf_ref[...].astype(BF16), wf_ref[...], preferred_element_type=F32)
    acc, ssq = None, None
    for c in range(D_INNER // tk):
        cols = slice(c * tk, (c + 1) * tk)
        z_ref = z0_ref if c * tk < P_TILE else z1_ref
        z = z_ref[:, (c * tk) % P_TILE:(c * tk) % P_TILE + tk].astype(F32)
        gated = y_ref[:, cols].astype(F32) * (z * jax.nn.sigmoid(z))
        sq = jnp.sum(gated * gated, axis=-1, keepdims=True)
        part = jnp.dot((gated * ng_ref[:, cols]).astype(BF16), wm_ref[cols, :], preferred_element_type=F32)
        acc = part if acc is None else acc + part
        ssq = sq if ssq is None else ssq + sq
    m = acc * lax.rsqrt(ssq * (1.0 / D_INNER) + EPS)
    merged = (jax.nn.sigmoid(gf_ref[...].astype(F32)) * f
              + jax.nn.sigmoid(gm_ref[...].astype(F32)) * m).astype(BF16)
    out = jnp.dot(merged, wo_ref[...], preferred_element_type=F32)
    o_ref[...] = x_ref[...] + _rms(out, post_ref[...])


def _merge_block(x, yf, y_ssd, p_main, norm_g, w_f, w_m, w_o, post_g, *, tm=512, tk=256):
    t = x.shape[0]
    assert t % tm == 0 and P_TILE == D_MODEL and P_TILE % tk == 0
    tile = lambda k: pl.BlockSpec((tm, P_TILE), lambda i: (i, k))
    return pl.pallas_call(
        functools.partial(_merge_body, tk=tk),
        grid=(t // tm,),
        in_specs=[pl.BlockSpec((tm, D_MODEL), lambda i: (i, 0)),
                  pl.BlockSpec((tm, D_FOURIER), lambda i: (i, 0)),
                  pl.BlockSpec((tm, D_INNER), lambda i: (i, 0)),
                  tile(P_PLAIN[0]), tile(P_PLAIN[1]), tile(P_PLAIN[2]), tile(P_PLAIN[3]),
                  _resident((1, D_INNER)),
                  _resident((D_FOURIER, D_MODEL)), _resident((D_INNER, D_MODEL)),
                  _resident((D_MODEL, D_MODEL)), _resident((1, D_MODEL))],
        out_specs=pl.BlockSpec((tm, D_MODEL), lambda i: (i, 0)),
        out_shape=jax.ShapeDtypeStruct((t, D_MODEL), F32),
        compiler_params=_params(("parallel",)),
        name="merge_out",
    )(x, yf, y_ssd, p_main, p_main, p_main, p_main, norm_g, w_f, w_m, w_o, post_g)


def _prepare(ffn1_pre_g, ffn1_post_g, ffn1_w_in, ffn1_w_out, mix_pre_g, mix_post_g, w_in, conv_w,
             conv_b, dt_bias, a_log, d_skip, ssm_norm_g, w_branch_f, w_branch_m, w_out,
             ffn2_pre_g, ffn2_post_g, ffn2_w_in, ffn2_w_out):
    row = lambda v: v.reshape(1, -1).astype(F32)
    off_f, off_z = D_FOURIER, D_FOURIER + D_INNER
    off_xbc = off_z + CONV_DIM
    off_dt = off_xbc + 2 * SSM_HEADS
    assert off_z == W_IN_CONV_TILE0 * P_TILE
    w_all = w_in.astype(BF16)
    w_plain = jnp.concatenate([w_all[:, off_f:off_z], w_all[:, off_dt:]], axis=1)
    pad = DT_PAD - 2 * SSM_HEADS
    w_dt = jnp.pad(w_all[:, off_xbc:off_dt], ((0, 0), (0, pad)))
    return dict(
        ffn1=(row(ffn1_pre_g), row(ffn1_post_g), ffn1_w_in.astype(BF16), ffn1_w_out.astype(BF16)),
        ffn2=(row(ffn2_pre_g), row(ffn2_post_g), ffn2_w_in.astype(BF16), ffn2_w_out.astype(BF16)),
        mix_pre=row(mix_pre_g), mix_post=row(mix_post_g),
        w_all=w_all, w_plain=w_plain, w_dt=w_dt,
        conv_w=conv_w.astype(F32), conv_b=row(conv_b),
        dt_bias=jnp.pad(row(dt_bias), ((0, 0), (0, pad))),
        a_log=jnp.pad(row(a_log), ((0, 0), (0, pad))),
        d_skip=row(jnp.repeat(d_skip, SSM_HEAD_DIM)), norm_g=row(ssm_norm_g),
        w_f=w_branch_f.astype(BF16), w_m=w_branch_m.astype(BF16), w_o=w_out.astype(BF16))


def _encoder_layer(x, p):
    b, l, d = x.shape
    t = b * l
    x = _ffn_block(x.reshape(t, d), *p["ffn1"])
    p_main, u_f, dt_raw = _proj_main(x, p["mix_pre"], p["w_plain"], p["w_all"], W_IN_CONV_TILE0, p["w_dt"],
                                     p["conv_w"], p["conv_b"], b, l)
    p_main = p_main.reshape(b, l, P_MAIN)
    dt_raw = dt_raw.reshape(b, l, DT_PAD)
    yf = _fourier_branch(u_f, l)
    prep = _ssd_prep(dt_raw, p["dt_bias"], p["a_log"])
    y_fwd = _ssd_pass(p_main, prep, 0)
    y_ssd = _ssd_pass(p_main, prep, 1, final_inputs=(y_fwd, p["d_skip"]))
    x = _merge_block(x, yf.reshape(t, D_FOURIER), y_ssd.reshape(t, D_INNER), p_main.reshape(t, P_MAIN),
                     p["norm_g"], p["w_f"], p["w_m"], p["w_o"], p["mix_post"])
    x = _ffn_block(x, *p["ffn2"])
    return x.reshape(b, l, d)


def kernel(x_prompt, x_sample, ffn1_pre_g, ffn1_post_g, ffn1_w_in, ffn1_w_out, mix_pre_g, mix_post_g, w_in, conv_w, conv_b, dt_bias, a_log, d_skip, ssm_norm_g, w_branch_f, w_branch_m, w_out, ffn2_pre_g, ffn2_post_g, ffn2_w_in, ffn2_w_out):
    layers = (ffn1_pre_g, ffn1_post_g, ffn1_w_in, ffn1_w_out, mix_pre_g, mix_post_g, w_in, conv_w,
              conv_b, dt_bias, a_log, d_skip, ssm_norm_g, w_branch_f, w_branch_m, w_out,
              ffn2_pre_g, ffn2_post_g, ffn2_w_in, ffn2_w_out)
    y_prompt, y_sample = x_prompt, x_sample
    for layer in range(ffn1_pre_g.shape[0]):
        p = _prepare(*(a[layer] for a in layers))
        y_prompt = _encoder_layer(y_prompt, p)
        y_sample = _encoder_layer(y_sample, p)
    return (y_prompt, y_sample)
```
